```python
import jax, jax.numpy as jnp
from jax import lax
import numpy as np

D_MODEL = 1024
BATCH = 8
SEQ = 4096
DEPTH = 1

PLE_DIM = 256
EPS = 1e-6
NEG = -1e30

GDN_HEADS = 4
GDN_DK = 128
GDN_DV = 128
GDN_CONV = 4
GDN_CHUNK = 64

NSA_HEADS = 8
NSA_GROUPS = 2
NSA_HPG = NSA_HEADS // NSA_GROUPS
NSA_DK = 64
NSA_DV = 64
CMP_LEN = 32
CMP_STRIDE = 16
CMP_HIDDEN = 256
SEL_BLOCK = 64
SEL_TOPN = 16
SEL_QCHUNK = 64
WINDOW = 512
WIN_QBLOCK = 128
FORCED_SCORE = 1e9

D_FF = 2816
FFN_CONV = 3

IN_SPLITS = (
    GDN_HEADS * GDN_DK,
    GDN_HEADS * GDN_DK,
    GDN_HEADS * GDN_DV,
    GDN_HEADS * GDN_DV,
    GDN_HEADS,
    GDN_HEADS,
    NSA_HEADS * NSA_DK,
    NSA_GROUPS * NSA_DK,
    NSA_GROUPS * NSA_DV,
    NSA_GROUPS * NSA_DK,
    NSA_GROUPS * NSA_DV,
    NSA_GROUPS * NSA_DK,
    NSA_GROUPS * NSA_DV,
    NSA_HEADS * 3,
    2 * D_MODEL,
)
D_IN = sum(IN_SPLITS)

kernel_name = 'hybrid_gdn_nsa_convffn_layer'


def rmsnorm(x, w):
    xf = x.astype(jnp.float32)
    y = xf * lax.rsqrt(jnp.mean(xf * xf, axis=-1, keepdims=True) + EPS)
    return (y * w.astype(jnp.float32)).astype(x.dtype)


def l2norm(x):
    return x * lax.rsqrt(jnp.sum(x * x, axis=-1, keepdims=True) + EPS)


def causal_dwconv(x, w):
    width, c = w.shape
    return lax.conv_general_dilated(
        x, w[:, None, :].astype(x.dtype), window_strides=(1,), padding=((width - 1, 0),),
        dimension_numbers=('NWC', 'WIO', 'NWC'), feature_group_count=c)


def alibi_slopes(n):
    return 2.0 ** (-8.0 * jnp.arange(1, n + 1, dtype=jnp.float32) / n)


def masked_softmax(s, valid):
    p = jax.nn.softmax(jnp.where(valid, s, NEG), axis=-1)
    return jnp.where(valid, p, 0.0)


def chunk_gated_delta_rule(q, k, v, beta, g):
    B, S, H, dk = q.shape
    dv = v.shape[-1]
    C = GDN_CHUNK
    n = S // C

    def chunks(t):
        t = t.reshape((B, n, C, H) + t.shape[3:])
        return jnp.moveaxis(t, (1, 3), (0, 2))

    qc, kc, vc, bc = chunks(q), chunks(k), chunks(v), chunks(beta)
    gcum = jnp.cumsum(chunks(g), axis=-1)
    idx = jnp.arange(C)
    strict = idx[:, None] > idx[None, :]
    incl = idx[:, None] >= idx[None, :]
    gdiff = gcum[..., :, None] - gcum[..., None, :]
    dec_strict = jnp.where(strict, jnp.exp(jnp.where(strict, gdiff, 0.0)), 0.0)
    dec_incl = jnp.where(incl, jnp.exp(jnp.where(incl, gdiff, 0.0)), 0.0)
    lower = bc[..., None] * jnp.einsum('nbhid,nbhjd->nbhij', kc, kc) * dec_strict
    gam = jnp.exp(gcum)
    rhs = jnp.concatenate([(bc * gam)[..., None] * kc, bc[..., None] * vc], axis=-1)
    sol = lax.linalg.triangular_solve(lower + jnp.eye(C, dtype=q.dtype), rhs,
                                      left_side=True, lower=True, unit_diagonal=True)
    w_c, u0_c = sol[..., :dk], sol[..., dk:]
    qk = jnp.einsum('nbhid,nbhjd->nbhij', qc, kc) * dec_incl
    q_dec = qc * gam[..., None]
    k_tail = kc * jnp.exp(gcum[..., -1:] - gcum)[..., None]
    g_last = gam[..., -1]

    def step(state, inp):
        w_, u0_, qk_, qd_, kt_, gl_ = inp
        u = u0_ - jnp.einsum('bhcd,bhde->bhce', w_, state)
        o = jnp.einsum('bhcd,bhde->bhce', qd_, state) + jnp.einsum('bhij,bhje->bhie', qk_, u)
        new_state = gl_[..., None, None] * state + jnp.einsum('bhcd,bhce->bhde', kt_, u)
        return new_state, o

    s0 = jnp.zeros((B, H, dk, dv), q.dtype)
    _, o = lax.scan(step, s0, (w_c, u0_c, qk, q_dec, k_tail, g_last))
    return jnp.moveaxis(o, (0, 2), (1, 3)).reshape(B, S, H, dv)


def gdn_mixer(q, k, v, z, b, a, conv_w, a_log, dt_bias, norm_w):
    B, S, _ = q.shape
    H, dk, dv = GDN_HEADS, GDN_DK, GDN_DV
    f32 = jnp.float32
    qkv = jax.nn.silu(causal_dwconv(jnp.concatenate([q, k, v], axis=-1), conv_w))
    q, k, v = jnp.split(qkv, [H * dk, 2 * H * dk], axis=-1)
    q = l2norm(q.reshape(B, S, H, dk).astype(f32)) * (dk ** -0.5)
    k = l2norm(k.reshape(B, S, H, dk).astype(f32))
    v = v.reshape(B, S, H, dv).astype(f32)
    beta = jax.nn.sigmoid(b.astype(f32))
    g = -jnp.exp(a_log.astype(f32)) * jax.nn.softplus(a.astype(f32) + dt_bias.astype(f32))
    o = chunk_gated_delta_rule(q, k, v, beta, g).astype(z.dtype)
    o = rmsnorm(o, norm_w) * jax.nn.silu(z.reshape(B, S, H, dv))
    return o.reshape(B, S, H * dv)


def compress_blocks(x, pos, w1, w2):
    B, S, G, d = x.shape
    r = CMP_LEN // CMP_STRIDE
    xs = x.reshape(B, S // CMP_STRIDE, CMP_STRIDE, G, d)
    nc = S // CMP_STRIDE - r + 1
    blocks = jnp.concatenate([xs[:, i:i + nc] for i in range(r)], axis=2)
    blocks = blocks + pos[:, None, :]
    flat = jnp.moveaxis(blocks, 3, 2).reshape(B, nc, G, CMP_LEN * d)
    return jax.nn.gelu(flat @ w1) @ w2


def nsa_mixer(q, k_cmp, v_cmp, k_slc, v_slc, k_win, v_win, gate,
              cmp_pos_k, cmp_w1_k, cmp_w2_k, cmp_pos_v, cmp_w1_v, cmp_w2_v):
    B, S, _ = q.shape
    H, G, R, dk, dv = NSA_HEADS, NSA_GROUPS, NSA_HPG, NSA_DK, NSA_DV
    f32 = jnp.float32
    slopes = alibi_slopes(H).reshape(G, R)
    t = jnp.arange(S)
    qh = q.reshape(B, S, G, R, dk) * (dk ** -0.5)
    grp = lambda y, d: y.reshape(B, S, G, d)

    kc = compress_blocks(grp(k_cmp, dk), cmp_pos_k, cmp_w1_k, cmp_w2_k)
    vc = compress_blocks(grp(v_cmp, dv), cmp_pos_v, cmp_w1_v, cmp_w2_v)
    nc = kc.shape[1]
    blk_start = jnp.arange(nc) * CMP_STRIDE
    dist_c = t[:, None] - (blk_start + CMP_LEN - 1)[None, :]
    s_c = jnp.einsum('bsgrd,bngd->bgrsn', qh, kc).astype(f32) \
        - slopes[None, :, :, None, None] * dist_c.astype(f32)
    p_c = masked_softmax(s_c, dist_c >= 0)
    o_cmp = jnp.einsum('bgrsn,bngd->bsgrd', p_c.astype(vc.dtype), vc)

    ns = S // SEL_BLOCK
    sel_start = jnp.arange(ns) * SEL_BLOCK
    overlap = jnp.clip(jnp.minimum(blk_start[:, None] + CMP_LEN, sel_start[None, :] + SEL_BLOCK)
                       - jnp.maximum(blk_start[:, None], sel_start[None, :]), 0, None).astype(f32) / CMP_LEN
    imp = jnp.einsum('bgsn,nj->bgsj', p_c.sum(axis=2), overlap)
    cur = t // SEL_BLOCK
    j = jnp.arange(ns)
    forced = (j[None, :] == 0) | (j[None, :] == cur[:, None]) | (j[None, :] == cur[:, None] - 1)
    sel_valid = j[None, :] <= cur[:, None]
    imp = jnp.where(sel_valid, jnp.where(forced, FORCED_SCORE, imp), NEG)
    n_sel = min(SEL_TOPN, ns)
    _, sel_idx = lax.top_k(imp, n_sel)

    ks = k_slc.reshape(B, ns, SEL_BLOCK, G, dk).transpose(0, 3, 1, 2, 4)
    vs = v_slc.reshape(B, ns, SEL_BLOCK, G, dv).transpose(0, 3, 1, 2, 4)
    QC = SEL_QCHUNK
    nq = S // QC
    q_ch = qh.reshape(B, nq, QC, G, R, dk).transpose(1, 0, 3, 4, 2, 5)
    idx_ch = sel_idx.reshape(B, G, nq, QC, n_sel).transpose(2, 0, 1, 3, 4)
    t_ch = t.reshape(nq, QC)
    nk = n_sel * SEL_BLOCK

    def sel_chunk(args):
        qb, ib, tb = args
        flat = ib.reshape(B, G, QC * n_sel)[:, :, :, None, None]
        kg = jnp.take_along_axis(ks, flat, axis=2).reshape(B, G, QC, nk, dk)
        vg = jnp.take_along_axis(vs, flat, axis=2).reshape(B, G, QC, nk, dv)
        pos = (ib[..., None] * SEL_BLOCK + jnp.arange(SEL_BLOCK)).reshape(B, G, QC, nk)
        dist = tb[None, None, :, None] - pos
        s = jnp.einsum('bgrqd,bgqkd->bgrqk', qb, kg).astype(f32) \
            - slopes[None, :, :, None, None] * dist[:, :, None].astype(f32)
        pr = masked_softmax(s, (dist >= 0)[:, :, None])
        return jnp.einsum('bgrqk,bgqkd->bgrqd', pr.astype(vg.dtype), vg)

    o_slc = lax.map(sel_chunk, (q_ch, idx_ch, t_ch))
    o_slc = o_slc.transpose(1, 0, 4, 2, 3, 5).reshape(B, S, G, R, dv)

    QB = WIN_QBLOCK
    nb = S // QB
    nwb = WINDOW // QB

    def band(y, d):
        yb = y.reshape(B, nb, QB, G, d).transpose(1, 0, 3, 2, 4)
        yp = jnp.pad(yb, ((nwb, 0), (0, 0), (0, 0), (0, 0), (0, 0)))
        return jnp.concatenate([yp[i:i + nb] for i in range(nwb + 1)], axis=3)

    kband, vband = band(k_win, dk), band(v_win, dv)
    q_wb = qh.reshape(B, nb, QB, G, R, dk).transpose(1, 0, 3, 4, 2, 5)
    qpos = t.reshape(nb, QB)
    kpos = (jnp.arange(nb)[:, None] - nwb) * QB + jnp.arange((nwb + 1) * QB)[None, :]

    def win_block(args):
        qb, kb, vb, tq, tk = args
        dist = tq[:, None] - tk[None, :]
        valid = (dist >= 0) & (dist < WINDOW) & (tk[None, :] >= 0)
        s = jnp.einsum('bgrqd,bgkd->bgrqk', qb, kb).astype(f32) \
            - slopes[None, :, :, None, None] * dist.astype(f32)
        pr = masked_softmax(s, valid)
        return jnp.einsum('bgrqk,bgkd->bgrqd', pr.astype(vb.dtype), vb)

    o_win = lax.map(win_block, (q_wb, kband, vband, qpos, kpos))
    o_win = o_win.transpose(1, 0, 4, 2, 3, 5).reshape(B, S, G, R, dv)

    gates = jax.nn.sigmoid(gate).reshape(B, S, G, R, 3)
    o = gates[..., 0:1] * o_cmp + gates[..., 1:2] * o_slc + gates[..., 2:3] * o_win
    return o.reshape(B, S, H * dv)


def setup_inputs(seed: int = 0) -> dict:
    key = jax.random.key(seed)
    ks = jax.random.split(key, 32)
    L, D = DEPTH, D_MODEL
    f32 = jnp.float32

    def nrm(k, shape, fan_in):
        return jax.random.normal(k, shape, f32) * (fan_in ** -0.5)

    def gain(k, n):
        return 1.0 + 0.05 * jax.random.normal(k, (L, n), f32)

    dt = jnp.exp(jax.random.uniform(ks[6], (L, GDN_HEADS), f32, minval=np.log(1e-3), maxval=np.log(1e-1)))
    return {
        'x': jax.random.normal(ks[0], (BATCH, SEQ, D), f32),
        'p': jax.random.normal(ks[1], (DEPTH, BATCH, SEQ, PLE_DIM), f32),
        'norm_mix_pre': gain(ks[2], D),
        'w_in': nrm(ks[3], (L, D, D_IN), D),
        'conv_qkv': nrm(ks[4], (L, GDN_CONV, 2 * GDN_HEADS * GDN_DK + GDN_HEADS * GDN_DV), GDN_CONV),
        'a_log': jnp.log(jax.random.uniform(ks[5], (L, GDN_HEADS), f32, minval=1.0, maxval=16.0)),
        'dt_bias': dt + jnp.log(-jnp.expm1(-dt)),
        'gdn_norm': gain(ks[7], GDN_DV),
        'cmp_pos_k': 0.1 * jax.random.normal(ks[8], (L, CMP_LEN, NSA_DK), f32),
        'cmp_w1_k': nrm(ks[9], (L, CMP_LEN * NSA_DK, CMP_HIDDEN), CMP_LEN * NSA_DK),
        'cmp_w2_k': nrm(ks[10], (L, CMP_HIDDEN, NSA_DK), CMP_HIDDEN),
        'cmp_pos_v': 0.1 * jax.random.normal(ks[11], (L, CMP_LEN, NSA_DV), f32),
        'cmp_w1_v': nrm(ks[12], (L, CMP_LEN * NSA_DV, CMP_HIDDEN), CMP_LEN * NSA_DV),
        'cmp_w2_v': nrm(ks[13], (L, CMP_HIDDEN, NSA_DV), CMP_HIDDEN),
        'w_a2d': nrm(ks[14], (L, GDN_HEADS * GDN_DV, D), GDN_HEADS * GDN_DV),
        'w_b2d': nrm(ks[15], (L, NSA_HEADS * NSA_DV, D), NSA_HEADS * NSA_DV),
        'w_o': nrm(ks[16], (L, D, D), D),
        'norm_mix_post': gain(ks[17], D),
        'norm_ffn_pre': gain(ks[18], D),
        'w_up': nrm(ks[19], (L, D, 2 * D_FF), D),
        'conv_ffn': nrm(ks[20], (L, FFN_CONV, 2 * D_FF), FFN_CONV),
        'conv_ffn_b': 0.02 * jax.random.normal(ks[21], (L, 2 * D_FF), f32),
        'w_down': nrm(ks[22], (L, D_FF, D), D_FF),
        'norm_ffn_post': gain(ks[23], D),
        'w_ple': nrm(ks[24], (L, PLE_DIM, D), PLE_DIM),
        'w_ple_gate': nrm(ks[25], (L, D, D), D),
        'norm_ple_post': gain(ks[26], D),
    }


def reference(x, p, norm_mix_pre, w_in, conv_qkv, a_log, dt_bias, gdn_norm,
              cmp_pos_k, cmp_w1_k, cmp_w2_k, cmp_pos_v, cmp_w1_v, cmp_w2_v,
              w_a2d, w_b2d, w_o, norm_mix_post, norm_ffn_pre, w_up, conv_ffn, conv_ffn_b,
              w_down, norm_ffn_post, w_ple, w_ple_gate, norm_ple_post):
    offs = np.cumsum(IN_SPLITS)[:-1].tolist()
    h = x
    for i in range(DEPTH):
        u = rmsnorm(h, norm_mix_pre[i])
        (qa, ka, va, za, ba, aa, qb, kcm, vcm, ksl, vsl, kwi, vwi, gnsa, gmix) = \
            jnp.split(u @ w_in[i], offs, axis=-1)
        ya = gdn_mixer(qa, ka, va, za, ba, aa, conv_qkv[i], a_log[i], dt_bias[i], gdn_norm[i])
        yb = nsa_mixer(qb, kcm, vcm, ksl, vsl, kwi, vwi, gnsa,
                       cmp_pos_k[i], cmp_w1_k[i], cmp_w2_k[i], cmp_pos_v[i], cmp_w1_v[i], cmp_w2_v[i])
        g_a, g_b = jnp.split(jax.nn.sigmoid(gmix), 2, axis=-1)
        mixed = (g_a * (ya @ w_a2d[i]) + g_b * (yb @ w_b2d[i])) @ w_o[i]
        h = h + rmsnorm(mixed, norm_mix_post[i])
        f = rmsnorm(h, norm_ffn_pre[i]) @ w_up[i]
        f = causal_dwconv(f, conv_ffn[i]) + conv_ffn_b[i]
        f_gate, f_val = jnp.split(f, 2, axis=-1)
        f = (jax.nn.gelu(f_gate) * f_val) @ w_down[i]
        h = h + rmsnorm(f, norm_ffn_post[i])
        e = (p[i] @ w_ple[i]) * jax.nn.sigmoid(h @ w_ple_gate[i])
        h = h + rmsnorm(e, norm_ple_post[i])
    return h
```

```python
import functools

import numpy as np
import jax
import jax.numpy as jnp
from jax import lax
from jax.experimental import pallas as pl
from jax.experimental.pallas import tpu as pltpu

F32 = jnp.float32
BF16 = jnp.bfloat16
EPS = 1e-6
NEG = -1e30
HIGHEST = lax.Precision.HIGHEST

GDN_HEADS, GDN_DK, GDN_DV, GDN_CONV, GDN_CHUNK = 4, 128, 128, 4, 64
NSA_HEADS, NSA_GROUPS, NSA_DK, NSA_DV = 8, 2, 64, 64
NSA_HPG = NSA_HEADS // NSA_GROUPS
CMP_LEN, CMP_STRIDE, CMP_HIDDEN = 32, 16, 256
SEL_BLOCK, SEL_TOPN = 64, 16
WINDOW = 512
FORCED_SCORE = 1e9
FFN_CONV = 3
IN_SPLITS = (512, 512, 512, 512, 4, 4, 512, 128, 128, 128, 128, 128, 128, 24, 2048)

LANES = 128
SUBLANES = 8
VMEM_LIMIT = 56 * 1024 * 1024

TM_PROJ = 512
TC_GDN = 256
TQ_NSA = 128
TK_NSA = 128
TM_TAIL = 256
FC_TAIL = 256


def _log2(n):
    assert n & (n - 1) == 0
    return n.bit_length() - 1


def _dot(a, b, precision=None):
    return jnp.dot(a, b, preferred_element_type=F32, precision=precision)


def _dot_nt(a, b):
    return lax.dot_general(a, b, (((1,), (1,)), ((), ())), preferred_element_type=F32)


def _rms(x, w):
    return x * lax.rsqrt(jnp.mean(x * x, axis=-1, keepdims=True) + EPS) * w


def _sigmoid(x):
    return 1.0 / (1.0 + jnp.exp(-x))


def _silu(x):
    return x * _sigmoid(x)


def _gelu_tanh(x):
    return x * (0.5 * (1.0 + jnp.tanh(0.7978845608028654 * (x + 0.044715 * (x * x * x)))))


def _softplus(x):
    return jnp.maximum(x, 0.0) + jnp.log1p(jnp.exp(-jnp.abs(x)))


def _resident(shape):
    zeros = (0,) * len(shape)
    return pl.BlockSpec(shape, lambda *_: zeros, pipeline_mode=pl.Buffered(1))


PROJ_WIDTHS = (1536, 512, 1024, 256, 512, 2048, 128, 256)
PROJ_DTYPES = (F32, F32, BF16, F32, BF16, F32, F32, F32)
PROJ_COLS = 512


def _in_proj_kernel(x_ref, nw_ref, w_ref, *out_refs):
    x = x_ref[...]
    u = _rms(x, nw_ref[...]).astype(BF16)
    off = 0
    for o_ref, width in zip(out_refs, PROJ_WIDTHS):
        for c0 in range(0, width, PROJ_COLS):
            c1 = min(c0 + PROJ_COLS, width)
            o_ref[:, c0:c1] = _dot(u, w_ref[:, off + c0:off + c1]).astype(o_ref.dtype)
        off += width


def _prep_w_in(w_in):
    offs = np.cumsum((0,) + IN_SPLITS)
    qa, ka, va, za, ba, aa, qb, kcm, vcm, ksl, vsl, kwi, vwi, gnsa, gmix = [
        w_in[:, offs[i]:offs[i + 1]] for i in range(len(IN_SPLITS))]
    d = w_in.shape[0]
    zeros = lambda n: jnp.zeros((d, n), w_in.dtype)
    qb_pad = jnp.concatenate(
        [jnp.concatenate([qb[:, h * NSA_DK:(h + 1) * NSA_DK] * (NSA_DK ** -0.5), zeros(LANES - NSA_DK)], axis=1)
         for h in range(NSA_HEADS)], axis=1)
    grp = lambda y, g: y[:, g * NSA_DK:(g + 1) * NSA_DK]
    kv4 = jnp.concatenate([grp(ksl, 0), grp(vsl, 0), grp(ksl, 1), grp(vsl, 1),
                           grp(kwi, 0), grp(vwi, 0), grp(kwi, 1), grp(vwi, 1)], axis=1)
    small_a = jnp.concatenate([ba, aa, zeros(LANES - 2 * GDN_HEADS)], axis=1)
    ng = NSA_HPG * 3
    gate_b = jnp.concatenate([gnsa[:, :ng], zeros(LANES - ng), gnsa[:, ng:], zeros(LANES - ng)], axis=1)
    w = jnp.concatenate([qa, ka, va, za, qb_pad, kcm, vcm, kv4, gmix, small_a, gate_b], axis=1)
    assert w.shape[1] == sum(PROJ_WIDTHS)
    return w.astype(BF16)


def _in_proj(x2, norm_w, w_prep):
    t, d = x2.shape
    tm = min(TM_PROJ, t)
    n = w_prep.shape[1]
    return pl.pallas_call(
        _in_proj_kernel,
        grid=(t // tm,),
        in_specs=[pl.BlockSpec((tm, d), lambda i: (i, 0)),
                  _resident((1, d)),
                  _resident((d, n))],
        out_specs=[pl.BlockSpec((tm, wd), lambda i: (i, 0)) for wd in PROJ_WIDTHS],
        out_shape=[jax.ShapeDtypeStruct((t, wd), dt) for wd, dt in zip(PROJ_WIDTHS, PROJ_DTYPES)],
        compiler_params=pltpu.CompilerParams(dimension_semantics=("arbitrary",), vmem_limit_bytes=VMEM_LIMIT),
        name="in_proj",
    )(x2, norm_w.reshape(1, d), w_prep)


def _gdn_kernel(qkv_ref, z_ref, sm_ref, cw_ref, alog_ref, dtb_ref, nw_ref, o_ref,
                xbuf, qn, kn, vn, gcs, bts, state, *, tc):
    H, DK, C = GDN_HEADS, GDN_DK, GDN_CHUNK
    j = pl.program_id(1)

    @pl.when(j == 0)
    def _():
        xbuf[0:SUBLANES, :] = jnp.zeros((SUBLANES, xbuf.shape[1]), F32)
        state[...] = jnp.zeros(state.shape, F32)

    xbuf[SUBLANES:SUBLANES + tc, :] = qkv_ref[...]

    for blk in range(3 * H):
        cs = slice(blk * DK, (blk + 1) * DK)
        conv = cw_ref[GDN_CONV - 1:GDN_CONV, cs] * xbuf[SUBLANES:SUBLANES + tc, cs]
        for kk in range(GDN_CONV - 1):
            r0 = SUBLANES - (GDN_CONV - 1) + kk
            conv = conv + cw_ref[kk:kk + 1, cs] * xbuf[r0:r0 + tc, cs]
        act = _silu(conv)
        which, h = divmod(blk, H)
        if which == 0:
            qn[h] = act * lax.rsqrt(jnp.sum(act * act, axis=-1, keepdims=True) + EPS) * (DK ** -0.5)
        elif which == 1:
            kn[h] = act * lax.rsqrt(jnp.sum(act * act, axis=-1, keepdims=True) + EPS)
        else:
            vn[h] = act
    xbuf[0:SUBLANES, :] = xbuf[tc:tc + SUBLANES, :]

    sm = sm_ref[...]
    bts[...] = _sigmoid(sm)
    g_all = -jnp.exp(alog_ref[...]) * _softplus(sm + dtb_ref[...])
    ri = lax.broadcasted_iota(jnp.int32, (tc, tc), 0)
    ci = lax.broadcasted_iota(jnp.int32, (tc, tc), 1)
    same_chunk = (ri >> _log2(C)) == (ci >> _log2(C))
    block_tril = jnp.where(same_chunk & (ri >= ci), 1.0, 0.0).astype(F32)
    gcs[...] = _dot(block_tril, g_all, precision=HIGHEST)

    i64 = lax.broadcasted_iota(jnp.int32, (C, C), 0)
    j64 = lax.broadcasted_iota(jnp.int32, (C, C), 1)
    incl = i64 >= j64
    strict = i64 > j64
    eye = jnp.where(i64 == j64, 1.0, 0.0).astype(F32)
    nw = nw_ref[...]

    def chunk(c, carry):
        r0 = pl.multiple_of(c * C, C)
        rows = pl.ds(r0, C)
        gc = gcs[rows, :]
        bt = bts[rows, :]
        gc_t = gc.T
        bt_t = bt.T
        for h in range(H):
            gc_col = gc[:, H + h:H + h + 1]
            gc_row = gc_t[H + h:H + h + 1, :]
            b_col = bt[:, h:h + 1]
            b_row = bt_t[h:h + 1, :]
            g_last = gc[C - 1:C, H + h:H + h + 1]
            gdiff = gc_col - gc_row
            e = jnp.exp(jnp.where(incl, gdiff, 0.0))
            dec_incl = jnp.where(incl, e, 0.0)
            dec_strict = jnp.where(strict, e, 0.0)
            q = qn[h, rows, :]
            k = kn[h, rows, :]
            v = vn[h, rows, :]
            kb = k.astype(BF16)
            kkt = _dot_nt(kb, kb)
            qkt = _dot_nt(q.astype(BF16), kb)
            npow = -(b_col * kkt * dec_strict)
            acc = npow
            for _ in range(5):
                npow = _dot(npow, npow, precision=HIGHEST)
                acc = acc + npow + _dot(acc, npow, precision=HIGHEST)
            t_beta = ((eye + acc) * b_row).astype(BF16)
            gam = jnp.exp(gc_col)
            rhs = jnp.concatenate([k * gam, v], axis=1).astype(BF16)
            sol = _dot(t_beta, rhs)
            w_c, u0 = sol[:, :DK], sol[:, DK:]
            s_old = state[h]
            wq = jnp.concatenate([w_c, q * gam], axis=0).astype(BF16)
            ws = _dot(wq, s_old.astype(BF16))
            u = u0 - ws[:C]
            ub = u.astype(BF16)
            o = ws[C:] + _dot((qkt * dec_incl).astype(BF16), ub)
            k_tail = k * jnp.exp(g_last - gc_col)
            state[h] = jnp.exp(g_last) * s_old + _dot(k_tail.T.astype(BF16), ub)
            y = _rms(o, nw) * _silu(z_ref[rows, h * GDN_DV:(h + 1) * GDN_DV])
            o_ref[rows, h * GDN_DV:(h + 1) * GDN_DV] = y.astype(o_ref.dtype)
        return carry

    lax.fori_loop(0, tc // C, chunk, 0)


def _gdn(qkv, z, small_a, conv_w, a_log, dt_bias, norm_w, batch, seq):
    t = qkv.shape[0]
    tc = min(TC_GDN, seq)
    nj = seq // tc
    H = GDN_HEADS
    pad = lambda v: jnp.zeros((1, LANES), F32).at[0, H:2 * H].set(v.astype(F32))
    tok = lambda w: pl.BlockSpec((tc, w), lambda b, j: (b * nj + j, 0))
    return pl.pallas_call(
        functools.partial(_gdn_kernel, tc=tc),
        grid=(batch, nj),
        in_specs=[tok(3 * H * GDN_DK), tok(H * GDN_DV), tok(LANES),
                  _resident(conv_w.shape), _resident((1, LANES)), _resident((1, LANES)),
                  _resident((1, GDN_DV))],
        out_specs=tok(H * GDN_DV),
        out_shape=jax.ShapeDtypeStruct((t, H * GDN_DV), BF16),
        scratch_shapes=[pltpu.VMEM((tc + SUBLANES, 3 * H * GDN_DK), F32),
                        pltpu.VMEM((H, tc, GDN_DK), F32),
                        pltpu.VMEM((H, tc, GDN_DK), F32),
                        pltpu.VMEM((H, tc, GDN_DV), F32),
                        pltpu.VMEM((tc, LANES), F32),
                        pltpu.VMEM((tc, LANES), F32),
                        pltpu.VMEM((H, GDN_DK, GDN_DV), F32)],
        compiler_params=pltpu.CompilerParams(dimension_semantics=("arbitrary", "arbitrary"),
                                             vmem_limit_bytes=VMEM_LIMIT),
        name="gdn_mixer",
    )(qkv, z, small_a, conv_w, pad(a_log), pad(dt_bias), norm_w.reshape(1, GDN_DV))


def _compress_kernel(x_ref, pos_ref, w1_ref, w2_ref, o_ref, shift, *, nsub):
    half = CMP_STRIDE * NSA_DK
    x = x_ref[0, 0]
    pos = pos_ref[0]
    first = _dot((x + pos[:, :half]).astype(BF16), w1_ref[0, :half, :])
    second = _dot((x + pos[:, half:]).astype(BF16), w1_ref[0, half:, :])
    shift[0:nsub, :] = second
    shift[nsub:nsub + SUBLANES, :] = jnp.zeros((SUBLANES, CMP_HIDDEN), F32)
    hidden = first + shift[1:nsub + 1, :]
    y = _dot(_gelu_tanh(hidden).astype(BF16), w2_ref[0])
    row = lax.broadcasted_iota(jnp.int32, y.shape, 0)
    o_ref[0, 0] = jnp.where(row < nsub - 1, y, 0.0)


def _compress(cmp, pos_k, w1_k, w2_k, pos_v, w1_v, w2_v, batch, seq):
    G, d = NSA_GROUPS, NSA_DK
    nsub = seq // CMP_STRIDE
    xs = cmp.reshape(batch, nsub, CMP_STRIDE, 2 * G, d).transpose(0, 3, 1, 2, 4).reshape(batch, 2 * G, nsub, CMP_STRIDE * d)
    pos = jnp.stack([pos_k.reshape(1, -1), pos_v.reshape(1, -1)])
    w1 = jnp.stack([w1_k, w1_v]).astype(BF16)
    w2 = jnp.stack([w2_k, w2_v]).astype(BF16)
    out = pl.pallas_call(
        functools.partial(_compress_kernel, nsub=nsub),
        grid=(batch, 2 * G),
        in_specs=[pl.BlockSpec((1, 1, nsub, CMP_STRIDE * d), lambda b, c: (b, c, 0, 0)),
                  pl.BlockSpec((1, 1, CMP_LEN * d), lambda b, c: (c // G, 0, 0)),
                  pl.BlockSpec((1, CMP_LEN * d, CMP_HIDDEN), lambda b, c: (c // G, 0, 0)),
                  pl.BlockSpec((1, CMP_HIDDEN, d), lambda b, c: (c // G, 0, 0))],
        out_specs=pl.BlockSpec((1, 1, nsub, d), lambda b, c: (b, c, 0, 0)),
        out_shape=jax.ShapeDtypeStruct((batch, 2 * G, nsub, d), F32),
        scratch_shapes=[pltpu.VMEM((nsub + SUBLANES, CMP_HIDDEN), F32)],
        compiler_params=pltpu.CompilerParams(dimension_semantics=("arbitrary", "arbitrary"),
                                             vmem_limit_bytes=VMEM_LIMIT),
        name="nsa_compress",
    )(xs, pos, w1, w2)
    return jnp.concatenate([out[:, :G], out[:, G:]], axis=-1).astype(BF16)


def _nsa_kernel(q_ref, kcv_ref, ksel_ref, kwin_ref, gate_ref, slope_ref, o_ref, *, tq, seq):
    R, DK = NSA_HPG, NSA_DK
    nsub = seq // CMP_STRIDE
    nsel = seq // SEL_BLOCK
    topn = min(SEL_TOPN, nsel)
    t0 = pl.program_id(2) * tq

    lane = lax.broadcasted_iota(jnp.int32, (tq, LANES), 1)
    trow = lax.broadcasted_iota(jnp.int32, (tq, LANES), 0) + t0
    slopes = [slope_ref[0, r:r + 1, :] for r in range(R)]
    q_heads = [q_ref[:, r * LANES:(r + 1) * LANES] for r in range(R)]

    kcv = kcv_ref[0, 0]
    n_idx = lax.broadcasted_iota(jnp.int32, (tq, nsub), 1)
    t_c = lax.broadcasted_iota(jnp.int32, (tq, nsub), 0) + t0
    dist_c = t_c - (n_idx * CMP_STRIDE + CMP_LEN - 1)
    valid_c = dist_c >= 0
    dist_cf = dist_c.astype(F32)
    psum = jnp.zeros((tq, nsub), F32)
    o_cmp = []
    for r in range(R):
        s = _dot_nt(q_heads[r], kcv) - slopes[r][:, :1] * dist_cf
        s = jnp.where(valid_c, s, NEG)
        m = jnp.max(s, axis=-1, keepdims=True)
        e = jnp.where(valid_c, jnp.exp(s - m), 0.0)
        den = jnp.sum(e, axis=-1, keepdims=True)
        p = e / jnp.where(den > 0.0, den, 1.0)
        psum = psum + p
        o_cmp.append(_dot(p.astype(BF16), kcv))

    n_o = lax.broadcasted_iota(jnp.int32, (nsub, LANES), 0) * CMP_STRIDE
    j_o = lax.broadcasted_iota(jnp.int32, (nsub, LANES), 1)
    ov = jnp.maximum(jnp.minimum(n_o + CMP_LEN, j_o * SEL_BLOCK + SEL_BLOCK) - jnp.maximum(n_o, j_o * SEL_BLOCK), 0)
    ov = jnp.where(j_o < nsel, ov.astype(F32) * (1.0 / CMP_LEN), 0.0).astype(BF16)
    p_hi = psum.astype(BF16)
    p_lo = (psum - p_hi.astype(F32)).astype(BF16)
    imp = _dot(p_hi, ov) + _dot(p_lo, ov)
    cur = trow >> _log2(SEL_BLOCK)
    valid_s = lane <= cur
    forced = (lane == 0) | (lane == cur) | (lane == cur - 1)
    score = jnp.where(valid_s, jnp.where(forced, FORCED_SCORE, imp), -jnp.inf)
    lane_f = lane.astype(F32)
    sel = jnp.zeros((tq, LANES), F32)
    for _ in range(topn):
        m = jnp.max(score, axis=-1, keepdims=True)
        first = jnp.min(jnp.where(score == m, lane_f, float(LANES)), axis=-1, keepdims=True)
        hit = lane_f == first
        sel = jnp.where(hit, 1.0, sel)
        score = jnp.where(hit, -jnp.inf, score)
    sel = jnp.where(valid_s, sel, 0.0).astype(BF16)

    q4 = jnp.concatenate(q_heads, axis=0)
    slope4 = jnp.concatenate([jnp.broadcast_to(sl, (tq, LANES)) for sl in slopes], axis=0)
    trow4 = jnp.concatenate([trow] * R, axis=0)
    lane4 = jnp.concatenate([lane] * R, axis=0)
    e_row = lax.broadcasted_iota(jnp.int32, (LANES, TK_NSA), 0)
    e_col = lax.broadcasted_iota(jnp.int32, (LANES, TK_NSA), 1) >> _log2(SEL_BLOCK)

    def flash(kv_ref, lo, hi, selected):
        def body(kb, carry):
            m, l, acc = carry
            kv = kv_ref[pl.ds(pl.multiple_of(kb * TK_NSA, TK_NSA), TK_NSA), :]
            dist = trow4 - (kb * TK_NSA + lane4)
            if selected:
                expand = jnp.where(e_row == kb * (TK_NSA // SEL_BLOCK) + e_col, 1.0, 0.0).astype(BF16)
                picked = _dot(sel, expand)
                valid = (jnp.concatenate([picked] * R, axis=0) > 0.5) & (dist >= 0)
            else:
                valid = (dist >= 0) & (dist < WINDOW)
            s = _dot_nt(q4, kv) - slope4 * dist.astype(F32)
            s = jnp.where(valid, s, NEG)
            m_new = jnp.maximum(m, jnp.max(s, axis=-1, keepdims=True))
            alpha = jnp.exp(m - m_new)
            p = jnp.where(valid, jnp.exp(s - m_new), 0.0)
            l = alpha * l + jnp.sum(p, axis=-1, keepdims=True)
            acc = alpha * acc + _dot(p.astype(BF16), kv)
            return m_new, l, acc

        init = (jnp.full((R * tq, 1), NEG, F32), jnp.zeros((R * tq, 1), F32), jnp.zeros((R * tq, LANES), F32))
        _, l, acc = lax.fori_loop(lo, hi, body, init)
        return acc / l

    kb_hi = (t0 + tq) // TK_NSA
    o_sel = flash(ksel_ref, 0, kb_hi, True)
    o_win = flash(kwin_ref, jnp.maximum(t0 // TK_NSA - WINDOW // TK_NSA, 0), kb_hi, False)

    gates = _sigmoid(gate_ref[...])
    ys = []
    for r in range(R):
        rows = slice(r * tq, (r + 1) * tq)
        ys.append(gates[:, 3 * r:3 * r + 1] * o_cmp[r] + gates[:, 3 * r + 1:3 * r + 2] * o_sel[rows]
                  + gates[:, 3 * r + 2:3 * r + 3] * o_win[rows])
    for half in range(R // 2):
        pair = jnp.where(lane < DK, pltpu.roll(ys[2 * half], DK, axis=1), ys[2 * half + 1])
        o_ref[:, half * LANES:(half + 1) * LANES] = pair.astype(o_ref.dtype)


def _nsa_attention(qb, kcv, kv4, gate_b, batch, seq):
    t = qb.shape[0]
    G, R = NSA_GROUPS, NSA_HPG
    tq = min(TQ_NSA, seq)
    nq = seq // tq
    nsub = seq // CMP_STRIDE
    slopes = 2.0 ** (-8.0 * jnp.arange(1, NSA_HEADS + 1, dtype=F32) / NSA_HEADS)
    slope_rows = jnp.zeros((G, SUBLANES, LANES), F32).at[:, :R, :].set(
        jnp.broadcast_to(slopes.reshape(G, R, 1), (G, R, LANES)))
    return pl.pallas_call(
        functools.partial(_nsa_kernel, tq=tq, seq=seq),
        grid=(batch, G, nq),
        in_specs=[pl.BlockSpec((tq, R * LANES), lambda b, g, i: (b * nq + i, g)),
                  pl.BlockSpec((1, 1, nsub, LANES), lambda b, g, i: (b, g, 0, 0)),
                  pl.BlockSpec((seq, LANES), lambda b, g, i: (b, g)),
                  pl.BlockSpec((seq, LANES), lambda b, g, i: (b, G + g)),
                  pl.BlockSpec((tq, LANES), lambda b, g, i: (b * nq + i, g)),
                  pl.BlockSpec((1, SUBLANES, LANES), lambda b, g, i: (g, 0, 0))],
        out_specs=pl.BlockSpec((tq, R * NSA_DV), lambda b, g, i: (b * nq + i, g)),
        out_shape=jax.ShapeDtypeStruct((t, NSA_HEADS * NSA_DV), BF16),
        compiler_params=pltpu.CompilerParams(dimension_semantics=("arbitrary", "arbitrary", "arbitrary"),
                                             vmem_limit_bytes=VMEM_LIMIT),
        name="nsa_attention",
    )(qb, kcv, kv4, kv4, gate_b, slope_rows)


def _tail_kernel(x_ref, ya_ref, yb_ref, gm_ref, p_ref, wa_ref, wb_ref, wo_ref, nmix_ref, nfpre_ref,
                 wup_ref, cw_ref, cb_ref, wdn_ref, nfpost_ref, wple_ref, wpg_ref, nple_ref, o_ref,
                 ext, tail, *, tm, seq, d_ff):
    d = x_ref.shape[1]
    fc2 = 2 * FC_TAIL
    halo = FFN_CONV - 1

    @pl.when((pl.program_id(0) * tm) % seq == 0)
    def _():
        tail[...] = jnp.zeros(tail.shape, F32)

    gm = gm_ref[...]
    mixed = _sigmoid(gm[:, :d]) * _dot(ya_ref[...], wa_ref[...]) + _sigmoid(gm[:, d:]) * _dot(yb_ref[...], wb_ref[...])
    h1 = x_ref[...] + _rms(_dot(mixed.astype(BF16), wo_ref[...]), nmix_ref[...])

    u = _rms(h1, nfpre_ref[...]).astype(BF16)
    acc = jnp.zeros((tm, d), F32)
    for c in range(d_ff // FC_TAIL):
        cols = slice(c * fc2, (c + 1) * fc2)
        ext[0:SUBLANES, :] = tail[:, cols]
        ext[SUBLANES:SUBLANES + tm, :] = _dot(u, wup_ref[:, cols])
        f = cb_ref[:, cols] + cw_ref[halo:halo + 1, cols] * ext[SUBLANES:SUBLANES + tm, :]
        for k in range(halo):
            r0 = SUBLANES - halo + k
            f = f + cw_ref[k:k + 1, cols] * ext[r0:r0 + tm, :]
        tail[:, cols] = ext[tm:tm + SUBLANES, :]
        a = _gelu_tanh(f[:, :FC_TAIL]) * f[:, FC_TAIL:]
        acc = acc + _dot(a.astype(BF16), wdn_ref[c * FC_TAIL:(c + 1) * FC_TAIL, :])
    h2 = h1 + _rms(acc, nfpost_ref[...])

    e = _dot(p_ref[...].astype(BF16), wple_ref[...]) * _sigmoid(_dot(h2.astype(BF16), wpg_ref[...]))
    o_ref[...] = h2 + _rms(e, nple_ref[...])


def _interleave_ffn(a, d_ff):
    nchunk = d_ff // FC_TAIL
    lead = a.shape[:-1]
    return a.reshape(lead + (2, nchunk, FC_TAIL)).swapaxes(-3, -2).reshape(lead + (2 * d_ff,))


def _tail(x2, ya, yb, gmix, p2, w_a2d, w_b2d, w_o, n_mix, n_fpre, w_up, conv_ffn, conv_b, w_down, n_fpost,
          w_ple, w_ple_gate, n_ple, seq):
    t, d = x2.shape
    d_ff = w_down.shape[0]
    tm = min(TM_TAIL, seq)
    tok = lambda w: pl.BlockSpec((tm, w), lambda i: (i, 0))
    row = lambda v: v.reshape(1, -1)
    b16 = lambda w: w.astype(BF16)
    wup = b16(_interleave_ffn(w_up, d_ff))
    cw = _interleave_ffn(conv_ffn, d_ff)
    cb = _interleave_ffn(conv_b.reshape(1, -1), d_ff)
    consts = [b16(w_a2d), b16(w_b2d), b16(w_o), row(n_mix), row(n_fpre), wup, cw, cb, b16(w_down), row(n_fpost),
              b16(w_ple), b16(w_ple_gate), row(n_ple)]
    return pl.pallas_call(
        functools.partial(_tail_kernel, tm=tm, seq=seq, d_ff=d_ff),
        grid=(t // tm,),
        in_specs=[tok(d), tok(ya.shape[1]), tok(yb.shape[1]), tok(2 * d), tok(p2.shape[1])]
                 + [_resident(c.shape) for c in consts],
        out_specs=tok(d),
        out_shape=jax.ShapeDtypeStruct((t, d), F32),
        scratch_shapes=[pltpu.VMEM((tm + SUBLANES, 2 * FC_TAIL), F32),
                        pltpu.VMEM((SUBLANES, 2 * d_ff), F32)],
        compiler_params=pltpu.CompilerParams(dimension_semantics=("arbitrary",), vmem_limit_bytes=VMEM_LIMIT),
        name="tail_ffn",
    )(x2, ya, yb, gmix, p2, *consts)


def _layer(h2, p2, batch, seq, norm_mix_pre, w_in, conv_qkv, a_log, dt_bias, gdn_norm,
           cmp_pos_k, cmp_w1_k, cmp_w2_k, cmp_pos_v, cmp_w1_v, cmp_w2_v,
           w_a2d, w_b2d, w_o, norm_mix_post, norm_ffn_pre, w_up, conv_ffn, conv_ffn_b,
           w_down, norm_ffn_post, w_ple, w_ple_gate, norm_ple_post):
    qkv, z, qb, cmp, kv4, gmix, small_a, gate_b = _in_proj(h2, norm_mix_pre, _prep_w_in(w_in))
    ya = _gdn(qkv, z, small_a, conv_qkv, a_log, dt_bias, gdn_norm, batch, seq)
    kcv = _compress(cmp, cmp_pos_k, cmp_w1_k, cmp_w2_k, cmp_pos_v, cmp_w1_v, cmp_w2_v, batch, seq)
    yb = _nsa_attention(qb, kcv, kv4, gate_b, batch, seq)
    return _tail(h2, ya, yb, gmix, p2, w_a2d, w_b2d, w_o, norm_mix_post, norm_ffn_pre, w_up, conv_ffn,
                 conv_ffn_b, w_down, norm_ffn_post, w_ple, w_ple_gate, norm_ple_post, seq)


def kernel(x, p, norm_mix_pre, w_in, conv_qkv, a_log, dt_bias, gdn_norm, cmp_pos_k, cmp_w1_k, cmp_w2_k, cmp_pos_v, cmp_w1_v, cmp_w2_v, w_a2d, w_b2d, w_o, norm_mix_post, norm_ffn_pre, w_up, conv_ffn, conv_ffn_b, w_down, norm_ffn_post, w_ple, w_ple_gate, norm_ple_post):
    batch, seq, d = x.shape
    params = (norm_mix_pre, w_in, conv_qkv, a_log, dt_bias, gdn_norm, cmp_pos_k, cmp_w1_k, cmp_w2_k,
              cmp_pos_v, cmp_w1_v, cmp_w2_v, w_a2d, w_b2d, w_o, norm_mix_post, norm_ffn_pre, w_up,
              conv_ffn, conv_ffn_b, w_down, norm_ffn_post, w_ple, w_ple_gate, norm_ple_post)
    h = x.reshape(batch * seq, d)
    for i in range(p.shape[0]):
        h = _layer(h, p[i].reshape(batch * seq, -1), batch, seq, *[w[i] for w in params])
    return h.reshape(batch, seq, d)
```

```python
import functools

import numpy as np
import jax
import jax.numpy as jnp
from jax import lax
from jax.experimental import pallas as pl
from jax.experimental.pallas import tpu as pltpu

F32 = jnp.float32
BF16 = jnp.bfloat16
EPS = 1e-6
NEG = -1e30
HIGHEST = lax.Precision.HIGHEST

GDN_HEADS, GDN_DK, GDN_DV, GDN_CONV, GDN_CHUNK = 4, 128, 128, 4, 64
NSA_HEADS, NSA_GROUPS, NSA_DK, NSA_DV = 8, 2, 64, 64
NSA_HPG = NSA_HEADS // NSA_GROUPS
CMP_LEN, CMP_STRIDE, CMP_HIDDEN = 32, 16, 256
SEL_BLOCK, SEL_TOPN = 64, 16
WINDOW = 512
FORCED_SCORE = 1e9
FFN_CONV = 3
IN_SPLITS = (512, 512, 512, 512, 4, 4, 512, 128, 128, 128, 128, 128, 128, 24, 2048)

LANES = 128
SUBLANES = 8
VMEM_LIMIT = 56 * 1024 * 1024

TM_PROJ = 512
TC_GDN = 256
TQ_NSA = 256
TK_NSA = 128
TM_TAIL = 256
FC_TAIL = 256


def _log2(n):
    assert n & (n - 1) == 0
    return n.bit_length() - 1


def _dot(a, b, precision=None):
    return jnp.dot(a, b, preferred_element_type=F32, precision=precision)


def _dot_nt(a, b):
    return lax.dot_general(a, b, (((1,), (1,)), ((), ())), preferred_element_type=F32)


def _rms(x, w):
    return x * lax.rsqrt(jnp.mean(x * x, axis=-1, keepdims=True) + EPS) * w


def _sigmoid(x):
    return 1.0 / (1.0 + jnp.exp(-x))


def _silu(x):
    return x * _sigmoid(x)


def _gelu_tanh(x):
    return x * (0.5 * (1.0 + jnp.tanh(0.7978845608028654 * (x + 0.044715 * (x * x * x)))))


def _softplus(x):
    return jnp.maximum(x, 0.0) + jnp.log1p(jnp.exp(-jnp.abs(x)))


def _resident(shape):
    zeros = (0,) * len(shape)
    return pl.BlockSpec(shape, lambda *_: zeros, pipeline_mode=pl.Buffered(1))


PROJ_WIDTHS = (1536, 512, 1024, 256, 512, 2048, 128, 256)
PROJ_DTYPES = (F32, F32, BF16, F32, BF16, F32, F32, F32)
PROJ_COLS = 512


def _in_proj_kernel(x_ref, nw_ref, w_ref, *out_refs):
    x = x_ref[...]
    u = _rms(x, nw_ref[...]).astype(BF16)
    off = 0
    for o_ref, width in zip(out_refs, PROJ_WIDTHS):
        for c0 in range(0, width, PROJ_COLS):
            c1 = min(c0 + PROJ_COLS, width)
            o_ref[:, c0:c1] = _dot(u, w_ref[:, off + c0:off + c1]).astype(o_ref.dtype)
        off += width


def _prep_w_in(w_in):
    offs = np.cumsum((0,) + IN_SPLITS)
    qa, ka, va, za, ba, aa, qb, kcm, vcm, ksl, vsl, kwi, vwi, gnsa, gmix = [
        w_in[:, offs[i]:offs[i + 1]] for i in range(len(IN_SPLITS))]
    d = w_in.shape[0]
    zeros = lambda n: jnp.zeros((d, n), w_in.dtype)
    qb_pad = jnp.concatenate(
        [jnp.concatenate([qb[:, h * NSA_DK:(h + 1) * NSA_DK] * (NSA_DK ** -0.5), zeros(LANES - NSA_DK)], axis=1)
         for h in range(NSA_HEADS)], axis=1)
    grp = lambda y, g: y[:, g * NSA_DK:(g + 1) * NSA_DK]
    kv4 = jnp.concatenate([grp(ksl, 0), grp(vsl, 0), grp(ksl, 1), grp(vsl, 1),
                           grp(kwi, 0), grp(vwi, 0), grp(kwi, 1), grp(vwi, 1)], axis=1)
    small_a = jnp.concatenate([ba, aa, zeros(LANES - 2 * GDN_HEADS)], axis=1)
    ng = NSA_HPG * 3
    gate_b = jnp.concatenate([gnsa[:, :ng], zeros(LANES - ng), gnsa[:, ng:], zeros(LANES - ng)], axis=1)
    w = jnp.concatenate([qa, ka, va, za, qb_pad, kcm, vcm, kv4, gmix, small_a, gate_b], axis=1)
    assert w.shape[1] == sum(PROJ_WIDTHS)
    return w.astype(BF16)


def _in_proj(x2, norm_w, w_prep):
    t, d = x2.shape
    tm = min(TM_PROJ, t)
    n = w_prep.shape[1]
    return pl.pallas_call(
        _in_proj_kernel,
        grid=(t // tm,),
        in_specs=[pl.BlockSpec((tm, d), lambda i: (i, 0)),
                  _resident((1, d)),
                  _resident((d, n))],
        out_specs=[pl.BlockSpec((tm, wd), lambda i: (i, 0)) for wd in PROJ_WIDTHS],
        out_shape=[jax.ShapeDtypeStruct((t, wd), dt) for wd, dt in zip(PROJ_WIDTHS, PROJ_DTYPES)],
        compiler_params=pltpu.CompilerParams(dimension_semantics=("arbitrary",), vmem_limit_bytes=VMEM_LIMIT),
        name="in_proj",
    )(x2, norm_w.reshape(1, d), w_prep)


def _gdn_kernel(qkv_ref, z_ref, sm_ref, cw_ref, alog_ref, dtb_ref, nw_ref, o_ref,
                xbuf, qn, kn, vn, gcs, bts, state, *, tc):
    H, DK, C = GDN_HEADS, GDN_DK, GDN_CHUNK
    j = pl.program_id(1)

    @pl.when(j == 0)
    def _():
        xbuf[0:SUBLANES, :] = jnp.zeros((SUBLANES, xbuf.shape[1]), F32)
        state[...] = jnp.zeros(state.shape, F32)

    xbuf[SUBLANES:SUBLANES + tc, :] = qkv_ref[...]

    for blk in range(3 * H):
        cs = slice(blk * DK, (blk + 1) * DK)
        conv = cw_ref[GDN_CONV - 1:GDN_CONV, cs] * xbuf[SUBLANES:SUBLANES + tc, cs]
        for kk in range(GDN_CONV - 1):
            r0 = SUBLANES - (GDN_CONV - 1) + kk
            conv = conv + cw_ref[kk:kk + 1, cs] * xbuf[r0:r0 + tc, cs]
        act = _silu(conv)
        which, h = divmod(blk, H)
        if which == 0:
            qn[h] = act * lax.rsqrt(jnp.sum(act * act, axis=-1, keepdims=True) + EPS) * (DK ** -0.5)
        elif which == 1:
            kn[h] = act * lax.rsqrt(jnp.sum(act * act, axis=-1, keepdims=True) + EPS)
        else:
            vn[h] = act
    xbuf[0:SUBLANES, :] = xbuf[tc:tc + SUBLANES, :]

    sm = sm_ref[...]
    bts[...] = _sigmoid(sm)
    g_all = -jnp.exp(alog_ref[...]) * _softplus(sm + dtb_ref[...])
    ri = lax.broadcasted_iota(jnp.int32, (tc, tc), 0)
    ci = lax.broadcasted_iota(jnp.int32, (tc, tc), 1)
    same_chunk = (ri >> _log2(C)) == (ci >> _log2(C))
    block_tril = jnp.where(same_chunk & (ri >= ci), 1.0, 0.0).astype(F32)
    gcs[...] = _dot(block_tril, g_all, precision=HIGHEST)

    i64 = lax.broadcasted_iota(jnp.int32, (C, C), 0)
    j64 = lax.broadcasted_iota(jnp.int32, (C, C), 1)
    incl = i64 >= j64
    strict = i64 > j64
    eye = jnp.where(i64 == j64, 1.0, 0.0).astype(F32)
    nw = nw_ref[...]

    def chunk(c, carry):
        r0 = pl.multiple_of(c * C, C)
        rows = pl.ds(r0, C)
        gc = gcs[rows, :]
        bt = bts[rows, :]
        gc_t = gc.T
        bt_t = bt.T
        heads = range(H)
        gc_col = [gc[:, H + h:H + h + 1] for h in heads]
        g_last = [gc[C - 1:C, H + h:H + h + 1] for h in heads]
        e = [jnp.exp(jnp.where(incl, gc_col[h] - gc_t[H + h:H + h + 1, :], 0.0)) for h in heads]
        q = [qn[h, rows, :] for h in heads]
        k = [kn[h, rows, :] for h in heads]
        kb = [k[h].astype(BF16) for h in heads]
        kkt = [_dot_nt(kb[h], kb[h]) for h in heads]
        qkt = [_dot_nt(q[h].astype(BF16), kb[h]) for h in heads]
        npow = [-(bt[:, h:h + 1] * kkt[h] * jnp.where(strict, e[h], 0.0)) for h in heads]
        acc = list(npow)
        for _ in range(5):
            npow = [_dot(npow[h], npow[h], precision=HIGHEST) for h in heads]
            acc = [acc[h] + npow[h] + _dot(acc[h], npow[h], precision=HIGHEST) for h in heads]
        t_beta = [((eye + acc[h]) * bt_t[h:h + 1, :]).astype(BF16) for h in heads]
        gam = [jnp.exp(gc_col[h]) for h in heads]
        sol = [_dot(t_beta[h], jnp.concatenate([k[h] * gam[h], vn[h, rows, :]], axis=1).astype(BF16)) for h in heads]
        s_old = [state[h] for h in heads]
        ws = [_dot(jnp.concatenate([sol[h][:, :DK], q[h] * gam[h]], axis=0).astype(BF16), s_old[h].astype(BF16))
              for h in heads]
        ub = [(sol[h][:, DK:] - ws[h][:C]).astype(BF16) for h in heads]
        o = [ws[h][C:] + _dot((qkt[h] * jnp.where(incl, e[h], 0.0)).astype(BF16), ub[h]) for h in heads]
        for h in heads:
            k_tail = k[h] * jnp.exp(g_last[h] - gc_col[h])
            state[h] = jnp.exp(g_last[h]) * s_old[h] + _dot(k_tail.T.astype(BF16), ub[h])
        for h in heads:
            y = _rms(o[h], nw) * _silu(z_ref[rows, h * GDN_DV:(h + 1) * GDN_DV])
            o_ref[rows, h * GDN_DV:(h + 1) * GDN_DV] = y.astype(o_ref.dtype)
        return carry

    lax.fori_loop(0, tc // C, chunk, 0)


def _gdn(qkv, z, small_a, conv_w, a_log, dt_bias, norm_w, batch, seq):
    t = qkv.shape[0]
    tc = min(TC_GDN, seq)
    nj = seq // tc
    H = GDN_HEADS
    pad = lambda v: jnp.zeros((1, LANES), F32).at[0, H:2 * H].set(v.astype(F32))
    tok = lambda w: pl.BlockSpec((tc, w), lambda b, j: (b * nj + j, 0))
    return pl.pallas_call(
        functools.partial(_gdn_kernel, tc=tc),
        grid=(batch, nj),
        in_specs=[tok(3 * H * GDN_DK), tok(H * GDN_DV), tok(LANES),
                  _resident(conv_w.shape), _resident((1, LANES)), _resident((1, LANES)),
                  _resident((1, GDN_DV))],
        out_specs=tok(H * GDN_DV),
        out_shape=jax.ShapeDtypeStruct((t, H * GDN_DV), BF16),
        scratch_shapes=[pltpu.VMEM((tc + SUBLANES, 3 * H * GDN_DK), F32),
                        pltpu.VMEM((H, tc, GDN_DK), F32),
                        pltpu.VMEM((H, tc, GDN_DK), F32),
                        pltpu.VMEM((H, tc, GDN_DV), F32),
                        pltpu.VMEM((tc, LANES), F32),
                        pltpu.VMEM((tc, LANES), F32),
                        pltpu.VMEM((H, GDN_DK, GDN_DV), F32)],
        compiler_params=pltpu.CompilerParams(dimension_semantics=("arbitrary", "arbitrary"),
                                             vmem_limit_bytes=VMEM_LIMIT),
        name="gdn_mixer",
    )(qkv, z, small_a, conv_w, pad(a_log), pad(dt_bias), norm_w.reshape(1, GDN_DV))


def _compress_kernel(x_ref, pos_ref, w1_ref, w2_ref, o_ref, shift, *, nsub):
    half = CMP_STRIDE * NSA_DK
    x = x_ref[0, 0]
    pos = pos_ref[0]
    first = _dot((x + pos[:, :half]).astype(BF16), w1_ref[0, :half, :])
    second = _dot((x + pos[:, half:]).astype(BF16), w1_ref[0, half:, :])
    shift[0:nsub, :] = second
    shift[nsub:nsub + SUBLANES, :] = jnp.zeros((SUBLANES, CMP_HIDDEN), F32)
    hidden = first + shift[1:nsub + 1, :]
    y = _dot(_gelu_tanh(hidden).astype(BF16), w2_ref[0])
    row = lax.broadcasted_iota(jnp.int32, y.shape, 0)
    o_ref[0, 0] = jnp.where(row < nsub - 1, y, 0.0)


def _compress(cmp, pos_k, w1_k, w2_k, pos_v, w1_v, w2_v, batch, seq):
    G, d = NSA_GROUPS, NSA_DK
    nsub = seq // CMP_STRIDE
    xs = cmp.reshape(batch, nsub, CMP_STRIDE, 2 * G, d).transpose(0, 3, 1, 2, 4).reshape(batch, 2 * G, nsub, CMP_STRIDE * d)
    pos = jnp.stack([pos_k.reshape(1, -1), pos_v.reshape(1, -1)])
    w1 = jnp.stack([w1_k, w1_v]).astype(BF16)
    w2 = jnp.stack([w2_k, w2_v]).astype(BF16)
    out = pl.pallas_call(
        functools.partial(_compress_kernel, nsub=nsub),
        grid=(batch, 2 * G),
        in_specs=[pl.BlockSpec((1, 1, nsub, CMP_STRIDE * d), lambda b, c: (b, c, 0, 0)),
                  pl.BlockSpec((1, 1, CMP_LEN * d), lambda b, c: (c // G, 0, 0)),
                  pl.BlockSpec((1, CMP_LEN * d, CMP_HIDDEN), lambda b, c: (c // G, 0, 0)),
                  pl.BlockSpec((1, CMP_HIDDEN, d), lambda b, c: (c // G, 0, 0))],
        out_specs=pl.BlockSpec((1, 1, nsub, d), lambda b, c: (b, c, 0, 0)),
        out_shape=jax.ShapeDtypeStruct((batch, 2 * G, nsub, d), F32),
        scratch_shapes=[pltpu.VMEM((nsub + SUBLANES, CMP_HIDDEN), F32)],
        compiler_params=pltpu.CompilerParams(dimension_semantics=("arbitrary", "arbitrary"),
                                             vmem_limit_bytes=VMEM_LIMIT),
        name="nsa_compress",
    )(xs, pos, w1, w2)
    return jnp.concatenate([out[:, :G], out[:, G:]], axis=-1).astype(BF16)


NSA_MASK = -1e30
NSA_M0 = -1e29
NSA_TILES_PER_STEP = 4
NSA_VROWS = NSA_DV + 16


def _nsa_kernel(q_ref, kcv_ref, ksel_ref, kwin_ref, gate_ref, slope_ref, o_ref,
                kc_aug, vc_t, ks_aug, vs_t, kw_aug, vw_t, negb, m_sc, acc_sc, *, tq, seq):
    R, DK, DV, TK = NSA_HPG, NSA_DK, NSA_DV, TK_NSA
    nsub = seq // CMP_STRIDE
    nsel = seq // SEL_BLOCK
    topn = min(SEL_TOPN, nsel)
    nql = R * tq
    i = pl.program_id(2)
    t0 = i * tq

    def aug_keys(kv, pos):
        lane = lax.broadcasted_iota(jnp.int32, kv.shape, 1)
        aug = jnp.where(lane == DK, (pos >> 7).astype(F32),
                        jnp.where(lane == DK + 1, (pos & 127).astype(F32), jnp.where(lane == DK + 2, 1.0, 0.0)))
        return jnp.where(lane < DK, kv, aug).astype(BF16)

    def values_t(kv):
        t = kv.T
        return jnp.concatenate([t[DK:DK + DV], jnp.ones((NSA_VROWS - DV, kv.shape[0]), F32)], axis=0).astype(BF16)

    @pl.when(i == 0)
    def _():
        def fill(c, carry):
            r0 = pl.multiple_of(c * TK, TK)
            pos = lax.broadcasted_iota(jnp.int32, (TK, LANES), 0) + r0
            for src, kdst, vdst in ((ksel_ref, ks_aug, vs_t), (kwin_ref, kw_aug, vw_t)):
                kv = src[pl.ds(r0, TK), :].astype(F32)
                kdst[pl.ds(r0, TK), :] = aug_keys(kv, pos)
                vdst[c] = values_t(kv)
            return carry

        lax.fori_loop(0, seq // TK, fill, 0)
        kcv = kcv_ref[0, 0].astype(F32)
        pos_c = lax.broadcasted_iota(jnp.int32, (nsub, LANES), 0) * CMP_STRIDE + (CMP_LEN - 1)
        kc_aug[...] = aug_keys(kcv, pos_c)
        vc_t[...] = kcv.T[DK:DK + DV].astype(BF16)

    row_q = lax.broadcasted_iota(jnp.int32, (LANES, tq), 0)
    t0_f = (jnp.zeros((LANES, tq), jnp.int32) + t0).astype(F32)
    qts = []
    for r in range(R):
        qt = q_ref[:, r * LANES:(r + 1) * LANES].astype(F32).T
        sl = slope_ref[0, r:r + 1, 0:1]
        aug = jnp.where(row_q == DK, sl * 128.0,
                        jnp.where(row_q == DK + 1, sl, jnp.where(row_q == DK + 2, -sl * t0_f, 0.0)))
        qts.append(jnp.where(row_q < DK, qt, aug))
    q_t = jnp.concatenate(qts, axis=1).astype(BF16)

    blk_end = lax.broadcasted_iota(jnp.int32, (nsub, nql), 0) * CMP_STRIDE + (CMP_LEN - 1)
    q_pos = (lax.broadcasted_iota(jnp.int32, (nsub, nql), 1) & (tq - 1)) + t0
    valid_c = blk_end <= q_pos
    s = jnp.where(valid_c, _dot(kc_aug[...], q_t), NSA_MASK)
    m = jnp.max(s, axis=0, keepdims=True)
    e = jnp.where(valid_c, jnp.exp(s - m), 0.0)
    den = jnp.sum(e, axis=0, keepdims=True)
    p = e * jnp.where(den > 0.0, 1.0 / den, 0.0)
    o_cmp = _dot(vc_t[...], p.astype(BF16))
    psum = p[:, 0:tq]
    for r in range(1, R):
        psum = psum + p[:, r * tq:(r + 1) * tq]

    j_o = lax.broadcasted_iota(jnp.int32, (nsel, nsub), 0) * SEL_BLOCK
    n_o = lax.broadcasted_iota(jnp.int32, (nsel, nsub), 1) * CMP_STRIDE
    ov = jnp.maximum(jnp.minimum(n_o + CMP_LEN, j_o + SEL_BLOCK) - jnp.maximum(n_o, j_o), 0)
    ov = (ov.astype(F32) * (1.0 / CMP_LEN)).astype(BF16)
    p_hi = psum.astype(BF16)
    p_lo = (psum - p_hi.astype(F32)).astype(BF16)
    imp = _dot(ov, p_hi) + _dot(ov, p_lo)
    blk = lax.broadcasted_iota(jnp.int32, (nsel, tq), 0)
    cur = (lax.broadcasted_iota(jnp.int32, (nsel, tq), 1) + t0) >> _log2(SEL_BLOCK)
    valid_s = blk <= cur
    forced = (blk == 0) | (blk == cur) | (blk == cur - 1)
    score = jnp.where(valid_s, jnp.where(forced, FORCED_SCORE, imp), -jnp.inf)
    blk_f = blk.astype(F32)
    sel = jnp.zeros((nsel, tq), F32)
    for _ in range(topn):
        m = jnp.max(score, axis=0, keepdims=True)
        first = jnp.min(jnp.where(score == m, blk_f, float(nsel)), axis=0, keepdims=True)
        hit = blk_f == first
        sel = jnp.where(hit, 1.0, sel)
        score = jnp.where(hit, -jnp.inf, score)
    nb = jnp.where(valid_s, jnp.where(sel > 0.5, 0.0, NSA_MASK), NSA_MASK)
    negb[...] = jnp.concatenate([nb] * R, axis=1)

    d_kq = (lax.broadcasted_iota(jnp.int32, (tq, nql), 0)
            - (lax.broadcasted_iota(jnp.int32, (tq, nql), 1) & (tq - 1)))
    qt = tq // TK
    first = i * qt

    def reset():
        m_sc[...] = jnp.full(m_sc.shape, NSA_M0, F32)
        acc_sc[...] = jnp.zeros(acc_sc.shape, F32)

    def scores(k_aug, tile0, ntiles):
        kt = k_aug[pl.ds(pl.multiple_of(tile0 * TK, TK), ntiles * TK), :]
        return _dot(kt, q_t)

    def accumulate(s, v_t, tile0):
        ntiles = s.shape[0] // TK
        m_old = m_sc[...]
        m_new = jnp.maximum(m_old, jnp.max(s, axis=0, keepdims=True))
        p = jnp.exp(s - m_new).astype(BF16)
        pv = _dot(v_t[tile0], p[:TK])
        for j in range(1, ntiles):
            pv = pv + _dot(v_t[tile0 + j], p[j * TK:(j + 1) * TK])
        acc_sc[...] = jnp.exp(m_old - m_new) * acc_sc[...] + pv
        m_sc[...] = m_new

    def finish():
        acc = acc_sc[...]
        return acc[:DV] / acc[DV:DV + 1]

    def picked(s, tile0):
        per_tile = TK // SEL_BLOCK
        parts = [s[h * SEL_BLOCK:(h + 1) * SEL_BLOCK] + negb[pl.ds(tile0 * per_tile + h, 1), :]
                 for h in range(s.shape[0] // SEL_BLOCK)]
        return jnp.concatenate(parts, axis=0)

    causal = lambda s: jnp.where(d_kq <= 0, s, NSA_MASK)
    window_edge = lambda s: jnp.where(d_kq > 0, s, NSA_MASK)

    def loop(lo, hi, fn):
        def body(kb, carry):
            fn(kb)
            return carry
        lax.fori_loop(lo, hi, body, 0)

    big = NSA_TILES_PER_STEP
    nbig = first >> _log2(big)
    reset()
    loop(0, nbig, lambda c: accumulate(picked(scores(ks_aug, c * big, big), c * big), vs_t, c * big))
    loop(nbig * big, first, lambda kb: accumulate(picked(scores(ks_aug, kb, 1), kb), vs_t, kb))
    accumulate(causal(picked(scores(ks_aug, first, qt), first)), vs_t, first)
    o_sel = finish()

    far = WINDOW // TK
    reset()

    @pl.when(first >= far)
    def _():
        s = scores(kw_aug, first - far, far + qt)
        s = jnp.concatenate([window_edge(s[:tq]), s[tq:far * TK], causal(s[far * TK:])], axis=0)
        accumulate(s, vw_t, first - far)

    @pl.when(first < far)
    def _():
        loop(0, first, lambda kb: accumulate(scores(kw_aug, kb, 1), vw_t, kb))
        accumulate(causal(scores(kw_aug, first, qt)), vw_t, first)

    o_win = finish()

    g_t = _sigmoid(gate_ref[...]).T
    ys = []
    for r in range(R):
        cs = slice(r * tq, (r + 1) * tq)
        ys.append(g_t[3 * r:3 * r + 1] * o_cmp[:, cs] + g_t[3 * r + 1:3 * r + 2] * o_sel[:, cs]
                  + g_t[3 * r + 2:3 * r + 3] * o_win[:, cs])
    for half in range(R // 2):
        pair = jnp.concatenate([ys[2 * half], ys[2 * half + 1]], axis=0)
        o_ref[:, half * LANES:(half + 1) * LANES] = pair.T.astype(o_ref.dtype)


def _nsa_attention(qb, kcv, kv4, gate_b, batch, seq):
    t = qb.shape[0]
    G, R = NSA_GROUPS, NSA_HPG
    tq = TQ_NSA
    assert tq % TK_NSA == 0 and WINDOW % tq == 0 and seq % tq == 0 and LANES == 2 * NSA_DK
    nq = seq // tq
    nsub = seq // CMP_STRIDE
    nsel = seq // SEL_BLOCK
    ntile = seq // TK_NSA
    slopes = 2.0 ** (-8.0 * jnp.arange(1, NSA_HEADS + 1, dtype=F32) / NSA_HEADS)
    slope_rows = jnp.zeros((G, SUBLANES, LANES), F32).at[:, :R, :].set(
        jnp.broadcast_to(slopes.reshape(G, R, 1), (G, R, LANES)))
    return pl.pallas_call(
        functools.partial(_nsa_kernel, tq=tq, seq=seq),
        grid=(batch, G, nq),
        in_specs=[pl.BlockSpec((tq, R * LANES), lambda b, g, i: (b * nq + i, g)),
                  pl.BlockSpec((1, 1, nsub, LANES), lambda b, g, i: (b, g, 0, 0)),
                  pl.BlockSpec((seq, LANES), lambda b, g, i: (b, g)),
                  pl.BlockSpec((seq, LANES), lambda b, g, i: (b, G + g)),
                  pl.BlockSpec((tq, LANES), lambda b, g, i: (b * nq + i, g)),
                  pl.BlockSpec((1, SUBLANES, LANES), lambda b, g, i: (g, 0, 0))],
        out_specs=pl.BlockSpec((tq, R * NSA_DV), lambda b, g, i: (b * nq + i, g)),
        out_shape=jax.ShapeDtypeStruct((t, NSA_HEADS * NSA_DV), BF16),
        scratch_shapes=[pltpu.VMEM((nsub, LANES), BF16),
                        pltpu.VMEM((NSA_DV, nsub), BF16),
                        pltpu.VMEM((seq, LANES), BF16),
                        pltpu.VMEM((ntile, NSA_VROWS, TK_NSA), BF16),
                        pltpu.VMEM((seq, LANES), BF16),
                        pltpu.VMEM((ntile, NSA_VROWS, TK_NSA), BF16),
                        pltpu.VMEM((nsel, R * tq), F32),
                        pltpu.VMEM((1, R * tq), F32),
                        pltpu.VMEM((NSA_VROWS, R * tq), F32)],
        compiler_params=pltpu.CompilerParams(dimension_semantics=("arbitrary", "arbitrary", "arbitrary"),
                                             vmem_limit_bytes=VMEM_LIMIT),
        name="nsa_attention",
    )(qb, kcv, kv4, kv4, gate_b, slope_rows)


def _tail_kernel(x_ref, ya_ref, yb_ref, gm_ref, p_ref, wa_ref, wb_ref, wo_ref, nmix_ref, nfpre_ref,
                 wup_ref, cw_ref, cb_ref, wdn_ref, nfpost_ref, wple_ref, wpg_ref, nple_ref, o_ref,
                 ext, tail, *, tm, seq, d_ff):
    d = x_ref.shape[1]
    fc2 = 2 * FC_TAIL
    halo = FFN_CONV - 1

    @pl.when((pl.program_id(0) * tm) % seq == 0)
    def _():
        tail[...] = jnp.zeros(tail.shape, F32)

    gm = gm_ref[...]
    mixed = _sigmoid(gm[:, :d]) * _dot(ya_ref[...], wa_ref[...]) + _sigmoid(gm[:, d:]) * _dot(yb_ref[...], wb_ref[...])
    h1 = x_ref[...] + _rms(_dot(mixed.astype(BF16), wo_ref[...]), nmix_ref[...])

    u = _rms(h1, nfpre_ref[...]).astype(BF16)
    acc = jnp.zeros((tm, d), F32)
    for c in range(d_ff // FC_TAIL):
        cols = slice(c * fc2, (c + 1) * fc2)
        ext[0:SUBLANES, :] = tail[:, cols]
        ext[SUBLANES:SUBLANES + tm, :] = _dot(u, wup_ref[:, cols])
        f = cb_ref[:, cols] + cw_ref[halo:halo + 1, cols] * ext[SUBLANES:SUBLANES + tm, :]
        for k in range(halo):
            r0 = SUBLANES - halo + k
            f = f + cw_ref[k:k + 1, cols] * ext[r0:r0 + tm, :]
        tail[:, cols] = ext[tm:tm + SUBLANES, :]
        a = _gelu_tanh(f[:, :FC_TAIL]) * f[:, FC_TAIL:]
        acc = acc + _dot(a.astype(BF16), wdn_ref[c * FC_TAIL:(c + 1) * FC_TAIL, :])
    h2 = h1 + _rms(acc, nfpost_ref[...])

    e = _dot(p_ref[...].astype(BF16), wple_ref[...]) * _sigmoid(_dot(h2.astype(BF16), wpg_ref[...]))
    o_ref[...] = h2 + _rms(e, nple_ref[...])


def _interleave_ffn(a, d_ff):
    nchunk = d_ff // FC_TAIL
    lead = a.shape[:-1]
    return a.reshape(lead + (2, nchunk, FC_TAIL)).swapaxes(-3, -2).reshape(lead + (2 * d_ff,))


def _tail(x2, ya, yb, gmix, p2, w_a2d, w_b2d, w_o, n_mix, n_fpre, w_up, conv_ffn, conv_b, w_down, n_fpost,
          w_ple, w_ple_gate, n_ple, seq):
    t, d = x2.shape
    d_ff = w_down.shape[0]
    tm = min(TM_TAIL, seq)
    tok = lambda w: pl.BlockSpec((tm, w), lambda i: (i, 0))
    row = lambda v: v.reshape(1, -1)
    b16 = lambda w: w.astype(BF16)
    wup = b16(_interleave_ffn(w_up, d_ff))
    cw = _interleave_ffn(conv_ffn, d_ff)
    cb = _interleave_ffn(conv_b.reshape(1, -1), d_ff)
    consts = [b16(w_a2d), b16(w_b2d), b16(w_o), row(n_mix), row(n_fpre), wup, cw, cb, b16(w_down), row(n_fpost),
              b16(w_ple), b16(w_ple_gate), row(n_ple)]
    return pl.pallas_call(
        functools.partial(_tail_kernel, tm=tm, seq=seq, d_ff=d_ff),
        grid=(t // tm,),
        in_specs=[tok(d), tok(ya.shape[1]), tok(yb.shape[1]), tok(2 * d), tok(p2.shape[1])]
                 + [_resident(c.shape) for c in consts],
        out_specs=tok(d),
        out_shape=jax.ShapeDtypeStruct((t, d), F32),
        scratch_shapes=[pltpu.VMEM((tm + SUBLANES, 2 * FC_TAIL), F32),
                        pltpu.VMEM((SUBLANES, 2 * d_ff), F32)],
        compiler_params=pltpu.CompilerParams(dimension_semantics=("arbitrary",), vmem_limit_bytes=VMEM_LIMIT),
        name="tail_ffn",
    )(x2, ya, yb, gmix, p2, *consts)


def _layer(h2, p2, batch, seq, norm_mix_pre, w_in, conv_qkv, a_log, dt_bias, gdn_norm,
           cmp_pos_k, cmp_w1_k, cmp_w2_k, cmp_pos_v, cmp_w1_v, cmp_w2_v,
           w_a2d, w_b2d, w_o, norm_mix_post, norm_ffn_pre, w_up, conv_ffn, conv_ffn_b,
           w_down, norm_ffn_post, w_ple, w_ple_gate, norm_ple_post):
    qkv, z, qb, cmp, kv4, gmix, small_a, gate_b = _in_proj(h2, norm_mix_pre, _prep_w_in(w_in))
    ya = _gdn(qkv, z, small_a, conv_qkv, a_log, dt_bias, gdn_norm, batch, seq)
    kcv = _compress(cmp, cmp_pos_k, cmp_w1_k, cmp_w2_k, cmp_pos_v, cmp_w1_v, cmp_w2_v, batch, seq)
    yb = _nsa_attention(qb, kcv, kv4, gate_b, batch, seq)
    return _tail(h2, ya, yb, gmix, p2, w_a2d, w_b2d, w_o, norm_mix_post, norm_ffn_pre, w_up, conv_ffn,
                 conv_ffn_b, w_down, norm_ffn_post, w_ple, w_ple_gate, norm_ple_post, seq)


def kernel(x, p, norm_mix_pre, w_in, conv_qkv, a_log, dt_bias, gdn_norm, cmp_pos_k, cmp_w1_k, cmp_w2_k, cmp_pos_v, cmp_w1_v, cmp_w2_v, w_a2d, w_b2d, w_o, norm_mix_post, norm_ffn_pre, w_up, conv_ffn, conv_ffn_b, w_down, norm_ffn_post, w_ple, w_ple_gate, norm_ple_post):
    batch, seq, d = x.shape
    params = (norm_mix_pre, w_in, conv_qkv, a_log, dt_bias, gdn_norm, cmp_pos_k, cmp_w1_k, cmp_w2_k,
              cmp_pos_v, cmp_w1_v, cmp_w2_v, w_a2d, w_b2d, w_o, norm_mix_post, norm_ffn_pre, w_up,
              conv_ffn, conv_ffn_b, w_down, norm_ffn_post, w_ple, w_ple_gate, norm_ple_post)
    h = x.reshape(batch * seq, d)
    for i in range(p.shape[0]):
        h = _layer(h, p[i].reshape(batch * seq, -1), batch, seq, *[w[i] for w in params])
    return h.reshape(batch, seq, d)
```

```python
import functools

import numpy as np
import jax
import jax.numpy as jnp
from jax import lax
from jax.experimental import pallas as pl
from jax.experimental.pallas import tpu as pltpu

F32 = jnp.float32
BF16 = jnp.bfloat16
EPS = 1e-6
NEG = -1e30
HIGHEST = lax.Precision.HIGHEST

GDN_HEADS, GDN_DK, GDN_DV, GDN_CONV, GDN_CHUNK = 4, 128, 128, 4, 64
NSA_HEADS, NSA_GROUPS, NSA_DK, NSA_DV = 8, 2, 64, 64
NSA_HPG = NSA_HEADS // NSA_GROUPS
CMP_LEN, CMP_STRIDE, CMP_HIDDEN = 32, 16, 256
SEL_BLOCK, SEL_TOPN = 64, 16
WINDOW = 512
FORCED_SCORE = 1e9
FFN_CONV = 3
IN_SPLITS = (512, 512, 512, 512, 4, 4, 512, 128, 128, 128, 128, 128, 128, 24, 2048)

LANES = 128
SUBLANES = 8
VMEM_LIMIT = 56 * 1024 * 1024

TM_PROJ = 512
TC_GDN = 256
TQ_NSA = 256
TK_NSA = 128
TM_TAIL = 256
FC_TAIL = 256


def _log2(n):
    assert n & (n - 1) == 0
    return n.bit_length() - 1


def _dot(a, b, precision=None):
    return jnp.dot(a, b, preferred_element_type=F32, precision=precision)


def _dot_nt(a, b):
    return lax.dot_general(a, b, (((1,), (1,)), ((), ())), preferred_element_type=F32)


def _rms(x, w):
    return x * lax.rsqrt(jnp.mean(x * x, axis=-1, keepdims=True) + EPS) * w


def _sigmoid(x):
    return 1.0 / (1.0 + jnp.exp(-x))


def _silu(x):
    return x * _sigmoid(x)


def _gelu_tanh(x):
    c = 0.7978845608028654
    half = 0.5 * x
    return half + half * jnp.tanh(x * (c + (c * 0.044715) * (x * x)))


def _softplus(x):
    return jnp.maximum(x, 0.0) + jnp.log1p(jnp.exp(-jnp.abs(x)))


def _resident(shape):
    zeros = (0,) * len(shape)
    return pl.BlockSpec(shape, lambda *_: zeros, pipeline_mode=pl.Buffered(1))


PROJ_WIDTHS = (1536, 512, 512, 256, 512, 2048, 128, 256)
PROJ_DTYPES = (F32, F32, BF16, F32, BF16, F32, F32, F32)
PROJ_COLS = 512


def _in_proj_kernel(x_ref, nw_ref, w_ref, *out_refs):
    x = x_ref[...]
    u = _rms(x, nw_ref[...]).astype(BF16)
    off = 0
    for o_ref, width in zip(out_refs, PROJ_WIDTHS):
        for c0 in range(0, width, PROJ_COLS):
            c1 = min(c0 + PROJ_COLS, width)
            o_ref[:, c0:c1] = _dot(u, w_ref[:, off + c0:off + c1]).astype(o_ref.dtype)
        off += width


def _prep_w_in(w_in):
    offs = np.cumsum((0,) + IN_SPLITS)
    qa, ka, va, za, ba, aa, qb, kcm, vcm, ksl, vsl, kwi, vwi, gnsa, gmix = [
        w_in[:, offs[i]:offs[i + 1]] for i in range(len(IN_SPLITS))]
    d = w_in.shape[0]
    zeros = lambda n: jnp.zeros((d, n), w_in.dtype)
    qb = qb * (NSA_DK ** -0.5)
    grp = lambda y, g: y[:, g * NSA_DK:(g + 1) * NSA_DK]
    kv4 = jnp.concatenate([grp(ksl, 0), grp(vsl, 0), grp(ksl, 1), grp(vsl, 1),
                           grp(kwi, 0), grp(vwi, 0), grp(kwi, 1), grp(vwi, 1)], axis=1)
    small_a = jnp.concatenate([ba, aa, zeros(LANES - 2 * GDN_HEADS)], axis=1)
    ng = NSA_HPG * 3
    gate_b = jnp.concatenate([gnsa[:, :ng], zeros(LANES - ng), gnsa[:, ng:], zeros(LANES - ng)], axis=1)
    w = jnp.concatenate([qa, ka, va, za, qb, kcm, vcm, kv4, gmix, small_a, gate_b], axis=1)
    assert w.shape[1] == sum(PROJ_WIDTHS)
    return w.astype(BF16)


def _in_proj(x2, norm_w, w_prep):
    t, d = x2.shape
    tm = min(TM_PROJ, t)
    n = w_prep.shape[1]
    return pl.pallas_call(
        _in_proj_kernel,
        grid=(t // tm,),
        in_specs=[pl.BlockSpec((tm, d), lambda i: (i, 0)),
                  _resident((1, d)),
                  _resident((d, n))],
        out_specs=[pl.BlockSpec((tm, wd), lambda i: (i, 0)) for wd in PROJ_WIDTHS],
        out_shape=[jax.ShapeDtypeStruct((t, wd), dt) for wd, dt in zip(PROJ_WIDTHS, PROJ_DTYPES)],
        compiler_params=pltpu.CompilerParams(dimension_semantics=("arbitrary",), vmem_limit_bytes=VMEM_LIMIT),
        name="in_proj",
    )(x2, norm_w.reshape(1, d), w_prep)


def _gdn_kernel(qkv_ref, z_ref, sm_ref, cw_ref, alog_ref, dtb_ref, nw_ref, o_ref,
                xbuf, qn, kn, vn, gcs, bts, state, *, tc):
    H, DK, C = GDN_HEADS, GDN_DK, GDN_CHUNK
    j = pl.program_id(1)

    @pl.when(j == 0)
    def _():
        xbuf[0:SUBLANES, :] = jnp.zeros((SUBLANES, xbuf.shape[1]), F32)
        state[...] = jnp.zeros(state.shape, F32)

    xbuf[SUBLANES:SUBLANES + tc, :] = qkv_ref[...]

    for blk in range(3 * H):
        cs = slice(blk * DK, (blk + 1) * DK)
        conv = cw_ref[GDN_CONV - 1:GDN_CONV, cs] * xbuf[SUBLANES:SUBLANES + tc, cs]
        for kk in range(GDN_CONV - 1):
            r0 = SUBLANES - (GDN_CONV - 1) + kk
            conv = conv + cw_ref[kk:kk + 1, cs] * xbuf[r0:r0 + tc, cs]
        act = _silu(conv)
        which, h = divmod(blk, H)
        if which == 0:
            qn[h] = act * lax.rsqrt(jnp.sum(act * act, axis=-1, keepdims=True) + EPS) * (DK ** -0.5)
        elif which == 1:
            kn[h] = act * lax.rsqrt(jnp.sum(act * act, axis=-1, keepdims=True) + EPS)
        else:
            vn[h] = act
    xbuf[0:SUBLANES, :] = xbuf[tc:tc + SUBLANES, :]

    sm = sm_ref[...]
    bts[...] = _sigmoid(sm)
    g_all = -jnp.exp(alog_ref[...]) * _softplus(sm + dtb_ref[...])
    ri = lax.broadcasted_iota(jnp.int32, (tc, tc), 0)
    ci = lax.broadcasted_iota(jnp.int32, (tc, tc), 1)
    same_chunk = (ri >> _log2(C)) == (ci >> _log2(C))
    block_tril = jnp.where(same_chunk & (ri >= ci), 1.0, 0.0).astype(F32)
    gcs[...] = _dot(block_tril, g_all, precision=HIGHEST)

    i64 = lax.broadcasted_iota(jnp.int32, (C, C), 0)
    j64 = lax.broadcasted_iota(jnp.int32, (C, C), 1)
    incl = i64 >= j64
    strict = i64 > j64
    eye = jnp.where(i64 == j64, 1.0, 0.0).astype(F32)
    pair_masks = [((i64 >> (lv + 1)) == (j64 >> (lv + 1))) & ((i64 >> lv) != (j64 >> lv)) for lv in range(_log2(C))]
    nw = nw_ref[...]

    units = [(c, h) for c in range(tc // C) for h in range(H)]
    rows = [slice(c * C, (c + 1) * C) for c in range(tc // C)]
    gc = [gcs[r, :] for r in rows]
    bt = [bts[r, :] for r in rows]
    gc_t = [x.T for x in gc]
    bt_t = [x.T for x in bt]
    gc_col = [gc[c][:, H + h:H + h + 1] for c, h in units]
    g_last = [gc[c][C - 1:C, H + h:H + h + 1] for c, h in units]
    e = [jnp.exp(jnp.where(incl, gc_col[n] - gc_t[c][H + h:H + h + 1, :], 0.0)) for n, (c, h) in enumerate(units)]
    q = [qn[h, rows[c], :] for c, h in units]
    k = [kn[h, rows[c], :] for c, h in units]
    kb = [x.astype(BF16) for x in k]
    kkt = [_dot_nt(x, x) for x in kb]
    qkm = [(_dot_nt(q[n].astype(BF16), kb[n]) * jnp.where(incl, e[n], 0.0)).astype(BF16) for n in range(len(units))]
    low = [bt[c][:, h:h + 1] * kkt[n] * jnp.where(strict, e[n], 0.0) for n, (c, h) in enumerate(units)]
    acc = [-jnp.where(pair_masks[0], x, 0.0) for x in low]
    for pm in pair_masks[1:]:
        off = [jnp.where(pm, x, 0.0) for x in low]
        tb = [off[n] + _dot(acc[n].astype(BF16), off[n].astype(BF16)) for n in range(len(units))]
        acc = [acc[n] - tb[n] - _dot(tb[n].astype(BF16), acc[n].astype(BF16)) for n in range(len(units))]
    t_beta = [((eye + acc[n]) * bt_t[c][h:h + 1, :]).astype(BF16) for n, (c, h) in enumerate(units)]
    gam = [jnp.exp(x) for x in gc_col]
    sol = [_dot(t_beta[n], jnp.concatenate([k[n] * gam[n], vn[h, rows[c], :]], axis=1).astype(BF16))
           for n, (c, h) in enumerate(units)]
    wq = [jnp.concatenate([sol[n][:, :DK], q[n] * gam[n]], axis=0).astype(BF16) for n in range(len(units))]
    kt_t = [(k[n] * jnp.exp(g_last[n] - gc_col[n])).T.astype(BF16) for n in range(len(units))]
    s_decay = [jnp.exp(x) for x in g_last]

    s_cur = [state[h] for h in range(H)]
    for n, (c, h) in enumerate(units):
        ws = _dot(wq[n], s_cur[h].astype(BF16))
        ub = (sol[n][:, DK:] - ws[:C]).astype(BF16)
        o = ws[C:] + _dot(qkm[n], ub)
        s_cur[h] = s_decay[n] * s_cur[h] + _dot(kt_t[n], ub)
        y = _rms(o, nw) * _silu(z_ref[rows[c], h * GDN_DV:(h + 1) * GDN_DV])
        o_ref[rows[c], h * GDN_DV:(h + 1) * GDN_DV] = y.astype(o_ref.dtype)
    for h in range(H):
        state[h] = s_cur[h]


def _gdn(qkv, z, small_a, conv_w, a_log, dt_bias, norm_w, batch, seq):
    t = qkv.shape[0]
    tc = min(TC_GDN, seq)
    nj = seq // tc
    H = GDN_HEADS
    pad = lambda v: jnp.zeros((1, LANES), F32).at[0, H:2 * H].set(v.astype(F32))
    tok = lambda w: pl.BlockSpec((tc, w), lambda b, j: (b * nj + j, 0))
    return pl.pallas_call(
        functools.partial(_gdn_kernel, tc=tc),
        grid=(batch, nj),
        in_specs=[tok(3 * H * GDN_DK), tok(H * GDN_DV), tok(LANES),
                  _resident(conv_w.shape), _resident((1, LANES)), _resident((1, LANES)),
                  _resident((1, GDN_DV))],
        out_specs=tok(H * GDN_DV),
        out_shape=jax.ShapeDtypeStruct((t, H * GDN_DV), BF16),
        scratch_shapes=[pltpu.VMEM((tc + SUBLANES, 3 * H * GDN_DK), F32),
                        pltpu.VMEM((H, tc, GDN_DK), F32),
                        pltpu.VMEM((H, tc, GDN_DK), F32),
                        pltpu.VMEM((H, tc, GDN_DV), F32),
                        pltpu.VMEM((tc, LANES), F32),
                        pltpu.VMEM((tc, LANES), F32),
                        pltpu.VMEM((H, GDN_DK, GDN_DV), F32)],
        compiler_params=pltpu.CompilerParams(dimension_semantics=("arbitrary", "arbitrary"),
                                             vmem_limit_bytes=VMEM_LIMIT),
        name="gdn_mixer",
    )(qkv, z, small_a, conv_w, pad(a_log), pad(dt_bias), norm_w.reshape(1, GDN_DV))


def _compress_kernel(x_ref, pos_ref, w1_ref, w2_ref, o_ref, shift, *, nsub):
    half = CMP_STRIDE * NSA_DK
    x = x_ref[0, 0]
    pos = pos_ref[0]
    first = _dot((x + pos[:, :half]).astype(BF16), w1_ref[0, :half, :])
    second = _dot((x + pos[:, half:]).astype(BF16), w1_ref[0, half:, :])
    shift[0:nsub, :] = second
    shift[nsub:nsub + SUBLANES, :] = jnp.zeros((SUBLANES, CMP_HIDDEN), F32)
    hidden = first + shift[1:nsub + 1, :]
    y = _dot(_gelu_tanh(hidden).astype(BF16), w2_ref[0])
    row = lax.broadcasted_iota(jnp.int32, y.shape, 0)
    o_ref[0, 0] = jnp.where(row < nsub - 1, y, 0.0)


def _compress(cmp, pos_k, w1_k, w2_k, pos_v, w1_v, w2_v, batch, seq):
    G, d = NSA_GROUPS, NSA_DK
    nsub = seq // CMP_STRIDE
    xs = cmp.reshape(batch, nsub, CMP_STRIDE, 2 * G, d).transpose(0, 3, 1, 2, 4).reshape(batch, 2 * G, nsub, CMP_STRIDE * d)
    pos = jnp.stack([pos_k.reshape(1, -1), pos_v.reshape(1, -1)])
    w1 = jnp.stack([w1_k, w1_v]).astype(BF16)
    w2 = jnp.stack([w2_k, w2_v]).astype(BF16)
    out = pl.pallas_call(
        functools.partial(_compress_kernel, nsub=nsub),
        grid=(batch, 2 * G),
        in_specs=[pl.BlockSpec((1, 1, nsub, CMP_STRIDE * d), lambda b, c: (b, c, 0, 0)),
                  pl.BlockSpec((1, 1, CMP_LEN * d), lambda b, c: (c // G, 0, 0)),
                  pl.BlockSpec((1, CMP_LEN * d, CMP_HIDDEN), lambda b, c: (c // G, 0, 0)),
                  pl.BlockSpec((1, CMP_HIDDEN, d), lambda b, c: (c // G, 0, 0))],
        out_specs=pl.BlockSpec((1, 1, nsub, d), lambda b, c: (b, c, 0, 0)),
        out_shape=jax.ShapeDtypeStruct((batch, 2 * G, nsub, d), F32),
        scratch_shapes=[pltpu.VMEM((nsub + SUBLANES, CMP_HIDDEN), F32)],
        compiler_params=pltpu.CompilerParams(dimension_semantics=("arbitrary", "arbitrary"),
                                             vmem_limit_bytes=VMEM_LIMIT),
        name="nsa_compress",
    )(xs, pos, w1, w2)
    return jnp.concatenate([out[:, :G], out[:, G:]], axis=-1).astype(BF16)


NSA_MASK = -1e30
NSA_M0 = -1e29
NSA_TILES_PER_STEP = 4
NSA_VROWS = NSA_DV + 16


def _nsa_kernel(q_ref, kcv_ref, ksel_ref, kwin_ref, gate_ref, slope_ref, o_ref,
                kc_aug, vc_t, ks_aug, vs_t, kw_aug, vw_t, negb, m_sc, acc_sc, *, tq, seq):
    R, DK, DV, TK = NSA_HPG, NSA_DK, NSA_DV, TK_NSA
    nsub = seq // CMP_STRIDE
    nsel = seq // SEL_BLOCK
    topn = min(SEL_TOPN, nsel)
    nql = R * tq
    i = pl.program_id(2)
    t0 = i * tq

    def aug_keys(kv, pos):
        lane = lax.broadcasted_iota(jnp.int32, kv.shape, 1)
        aug = jnp.where(lane == DK, (pos >> 7).astype(F32),
                        jnp.where(lane == DK + 1, (pos & 127).astype(F32), jnp.where(lane == DK + 2, 1.0, 0.0)))
        return jnp.where(lane < DK, kv, aug).astype(BF16)

    def values_t(kv):
        t = kv.T
        return jnp.concatenate([t[DK:DK + DV], jnp.ones((NSA_VROWS - DV, kv.shape[0]), F32)], axis=0).astype(BF16)

    @pl.when(i == 0)
    def _():
        def fill(c, carry):
            r0 = pl.multiple_of(c * TK, TK)
            pos = lax.broadcasted_iota(jnp.int32, (TK, LANES), 0) + r0
            for src, kdst, vdst in ((ksel_ref, ks_aug, vs_t), (kwin_ref, kw_aug, vw_t)):
                kv = src[pl.ds(r0, TK), :].astype(F32)
                kdst[pl.ds(r0, TK), :] = aug_keys(kv, pos)
                vdst[c] = values_t(kv)
            return carry

        lax.fori_loop(0, seq // TK, fill, 0)
        kcv = kcv_ref[0, 0].astype(F32)
        pos_c = lax.broadcasted_iota(jnp.int32, (nsub, LANES), 0) * CMP_STRIDE + (CMP_LEN - 1)
        kc_aug[...] = aug_keys(kcv, pos_c)
        vc_t[...] = kcv.T[DK:DK + DV].astype(BF16)

    row_q = lax.broadcasted_iota(jnp.int32, (LANES - DK, tq), 0)
    t0_f = (jnp.zeros((LANES - DK, tq), jnp.int32) + t0).astype(F32)
    q_all = q_ref[...].astype(F32).T
    qts = []
    for r in range(R):
        sl = slope_ref[0, r:r + 1, 0:1]
        aug = jnp.where(row_q == 0, sl * 128.0, jnp.where(row_q == 1, sl, jnp.where(row_q == 2, -sl * t0_f, 0.0)))
        qts.append(jnp.concatenate([q_all[r * DK:(r + 1) * DK], aug], axis=0))
    q_t = jnp.concatenate(qts, axis=1).astype(BF16)

    blk_end = lax.broadcasted_iota(jnp.int32, (nsub, nql), 0) * CMP_STRIDE + (CMP_LEN - 1)
    q_pos = (lax.broadcasted_iota(jnp.int32, (nsub, nql), 1) & (tq - 1)) + t0
    valid_c = blk_end <= q_pos
    s = jnp.where(valid_c, _dot(kc_aug[...], q_t), NSA_MASK)
    m = jnp.max(s, axis=0, keepdims=True)
    e = jnp.where(valid_c, jnp.exp(s - m), 0.0)
    den = jnp.sum(e, axis=0, keepdims=True)
    p = e * jnp.where(den > 0.0, 1.0 / den, 0.0)
    o_cmp = _dot(vc_t[...], p.astype(BF16))
    psum = p[:, 0:tq]
    for r in range(1, R):
        psum = psum + p[:, r * tq:(r + 1) * tq]

    j_o = lax.broadcasted_iota(jnp.int32, (nsel, nsub), 0) * SEL_BLOCK
    n_o = lax.broadcasted_iota(jnp.int32, (nsel, nsub), 1) * CMP_STRIDE
    ov = jnp.maximum(jnp.minimum(n_o + CMP_LEN, j_o + SEL_BLOCK) - jnp.maximum(n_o, j_o), 0)
    ov = (ov.astype(F32) * (1.0 / CMP_LEN)).astype(BF16)
    p_hi = psum.astype(BF16)
    p_lo = (psum - p_hi.astype(F32)).astype(BF16)
    imp = _dot(ov, p_hi) + _dot(ov, p_lo)
    blk = lax.broadcasted_iota(jnp.int32, (nsel, tq), 0)
    cur = (lax.broadcasted_iota(jnp.int32, (nsel, tq), 1) + t0) >> _log2(SEL_BLOCK)
    valid_s = blk <= cur
    forced = (blk == 0) | (blk == cur) | (blk == cur - 1)
    score = jnp.where(valid_s, jnp.where(forced, FORCED_SCORE, imp), -jnp.inf)
    blk_f = blk.astype(F32)
    sel = jnp.zeros((nsel, tq), F32)
    for _ in range(topn):
        m = jnp.max(score, axis=0, keepdims=True)
        first = jnp.min(jnp.where(score == m, blk_f, float(nsel)), axis=0, keepdims=True)
        hit = blk_f == first
        sel = jnp.where(hit, 1.0, sel)
        score = jnp.where(hit, -jnp.inf, score)
    nb = jnp.where(valid_s, jnp.where(sel > 0.5, 0.0, NSA_MASK), NSA_MASK)
    negb[...] = jnp.concatenate([nb] * R, axis=1)

    d_kq = (lax.broadcasted_iota(jnp.int32, (tq, nql), 0)
            - (lax.broadcasted_iota(jnp.int32, (tq, nql), 1) & (tq - 1)))
    qt = tq // TK
    first = i * qt

    def reset():
        m_sc[...] = jnp.full(m_sc.shape, NSA_M0, F32)
        acc_sc[...] = jnp.zeros(acc_sc.shape, F32)

    def scores(k_aug, tile0, ntiles):
        kt = k_aug[pl.ds(pl.multiple_of(tile0 * TK, TK), ntiles * TK), :]
        return _dot(kt, q_t)

    def accumulate(s, v_t, tile0):
        ntiles = s.shape[0] // TK
        m_old = m_sc[...]
        m_new = jnp.maximum(m_old, jnp.max(s, axis=0, keepdims=True))
        p = jnp.exp(s - m_new).astype(BF16)
        pv = _dot(v_t[tile0], p[:TK])
        for j in range(1, ntiles):
            pv = pv + _dot(v_t[tile0 + j], p[j * TK:(j + 1) * TK])
        acc_sc[...] = jnp.exp(m_old - m_new) * acc_sc[...] + pv
        m_sc[...] = m_new

    def finish():
        acc = acc_sc[...]
        return acc[:DV] / acc[DV:DV + 1]

    def picked(s, tile0):
        per_tile = TK // SEL_BLOCK
        parts = [s[h * SEL_BLOCK:(h + 1) * SEL_BLOCK] + negb[pl.ds(tile0 * per_tile + h, 1), :]
                 for h in range(s.shape[0] // SEL_BLOCK)]
        return jnp.concatenate(parts, axis=0)

    causal = lambda s: jnp.where(d_kq <= 0, s, NSA_MASK)
    window_edge = lambda s: jnp.where(d_kq > 0, s, NSA_MASK)

    def loop(lo, hi, fn):
        def body(kb, carry):
            fn(kb)
            return carry
        lax.fori_loop(lo, hi, body, 0)

    big = NSA_TILES_PER_STEP
    nbig = first >> _log2(big)
    reset()
    loop(0, nbig, lambda c: accumulate(picked(scores(ks_aug, c * big, big), c * big), vs_t, c * big))
    loop(nbig * big, first, lambda kb: accumulate(picked(scores(ks_aug, kb, 1), kb), vs_t, kb))
    accumulate(causal(picked(scores(ks_aug, first, qt), first)), vs_t, first)
    o_sel = finish()

    far = WINDOW // TK
    reset()

    @pl.when(first >= far)
    def _():
        s = scores(kw_aug, first - far, far + qt)
        s = jnp.concatenate([window_edge(s[:tq]), s[tq:far * TK], causal(s[far * TK:])], axis=0)
        accumulate(s, vw_t, first - far)

    @pl.when(first < far)
    def _():
        loop(0, first, lambda kb: accumulate(scores(kw_aug, kb, 1), vw_t, kb))
        accumulate(causal(scores(kw_aug, first, qt)), vw_t, first)

    o_win = finish()

    g_t = _sigmoid(gate_ref[...]).T
    ys = []
    for r in range(R):
        cs = slice(r * tq, (r + 1) * tq)
        ys.append(g_t[3 * r:3 * r + 1] * o_cmp[:, cs] + g_t[3 * r + 1:3 * r + 2] * o_sel[:, cs]
                  + g_t[3 * r + 2:3 * r + 3] * o_win[:, cs])
    for half in range(R // 2):
        pair = jnp.concatenate([ys[2 * half], ys[2 * half + 1]], axis=0)
        o_ref[:, half * LANES:(half + 1) * LANES] = pair.T.astype(o_ref.dtype)


def _nsa_attention(qb, kcv, kv4, gate_b, batch, seq):
    t = qb.shape[0]
    G, R = NSA_GROUPS, NSA_HPG
    tq = TQ_NSA
    assert tq % TK_NSA == 0 and WINDOW % tq == 0 and seq % tq == 0 and LANES == 2 * NSA_DK
    nq = seq // tq
    nsub = seq // CMP_STRIDE
    nsel = seq // SEL_BLOCK
    ntile = seq // TK_NSA
    slopes = 2.0 ** (-8.0 * jnp.arange(1, NSA_HEADS + 1, dtype=F32) / NSA_HEADS)
    slope_rows = jnp.zeros((G, SUBLANES, LANES), F32).at[:, :R, :].set(
        jnp.broadcast_to(slopes.reshape(G, R, 1), (G, R, LANES)))
    return pl.pallas_call(
        functools.partial(_nsa_kernel, tq=tq, seq=seq),
        grid=(batch, G, nq),
        in_specs=[pl.BlockSpec((tq, R * NSA_DK), lambda b, g, i: (b * nq + i, g)),
                  pl.BlockSpec((1, 1, nsub, LANES), lambda b, g, i: (b, g, 0, 0)),
                  pl.BlockSpec((seq, LANES), lambda b, g, i: (b, g)),
                  pl.BlockSpec((seq, LANES), lambda b, g, i: (b, G + g)),
                  pl.BlockSpec((tq, LANES), lambda b, g, i: (b * nq + i, g)),
                  pl.BlockSpec((1, SUBLANES, LANES), lambda b, g, i: (g, 0, 0))],
        out_specs=pl.BlockSpec((tq, R * NSA_DV), lambda b, g, i: (b * nq + i, g)),
        out_shape=jax.ShapeDtypeStruct((t, NSA_HEADS * NSA_DV), BF16),
        scratch_shapes=[pltpu.VMEM((nsub, LANES), BF16),
                        pltpu.VMEM((NSA_DV, nsub), BF16),
                        pltpu.VMEM((seq, LANES), BF16),
                        pltpu.VMEM((ntile, NSA_VROWS, TK_NSA), BF16),
                        pltpu.VMEM((seq, LANES), BF16),
                        pltpu.VMEM((ntile, NSA_VROWS, TK_NSA), BF16),
                        pltpu.VMEM((nsel, R * tq), F32),
                        pltpu.VMEM((1, R * tq), F32),
                        pltpu.VMEM((NSA_VROWS, R * tq), F32)],
        compiler_params=pltpu.CompilerParams(dimension_semantics=("arbitrary", "arbitrary", "arbitrary"),
                                             vmem_limit_bytes=VMEM_LIMIT),
        name="nsa_attention",
    )(qb, kcv, kv4, kv4, gate_b, slope_rows)


def _tail_kernel(x_ref, ya_ref, yb_ref, gm_ref, p_ref, wa_ref, wb_ref, wo_ref, nmix_ref, nfpre_ref,
                 wup_ref, cw_ref, cb_ref, wdn_ref, nfpost_ref, wple_ref, wpg_ref, nple_ref, o_ref,
                 ext, tail, act, *, tm, seq, d_ff):
    d = x_ref.shape[1]
    halo = FFN_CONV - 1

    @pl.when((pl.program_id(0) * tm) % seq == 0)
    def _():
        tail[...] = jnp.zeros(tail.shape, F32)

    gm = gm_ref[...]
    mixed = _sigmoid(gm[:, :d]) * _dot(ya_ref[...], wa_ref[...]) + _sigmoid(gm[:, d:]) * _dot(yb_ref[...], wb_ref[...])
    h1 = x_ref[...] + _rms(_dot(mixed.astype(BF16), wo_ref[...]), nmix_ref[...])

    u = _rms(h1, nfpre_ref[...]).astype(BF16)
    for c in range(d_ff // FC_TAIL):
        fs = []
        for part in range(2):
            cols = slice(part * d_ff + c * FC_TAIL, part * d_ff + (c + 1) * FC_TAIL)
            ext[0:SUBLANES, :] = tail[:, cols]
            ext[SUBLANES:SUBLANES + tm, :] = _dot(u, wup_ref[:, cols])
            f = cb_ref[:, cols] + cw_ref[halo:halo + 1, cols] * ext[SUBLANES:SUBLANES + tm, :]
            for k in range(halo):
                r0 = SUBLANES - halo + k
                f = f + cw_ref[k:k + 1, cols] * ext[r0:r0 + tm, :]
            tail[:, cols] = ext[tm:tm + SUBLANES, :]
            fs.append(f)
        act[:, c * FC_TAIL:(c + 1) * FC_TAIL] = (_gelu_tanh(fs[0]) * fs[1]).astype(BF16)
    h2 = h1 + _rms(_dot(act[...], wdn_ref[...]), nfpost_ref[...])

    e = _dot(p_ref[...].astype(BF16), wple_ref[...]) * _sigmoid(_dot(h2.astype(BF16), wpg_ref[...]))
    o_ref[...] = h2 + _rms(e, nple_ref[...])


def _tail(x2, ya, yb, gmix, p2, w_a2d, w_b2d, w_o, n_mix, n_fpre, w_up, conv_ffn, conv_b, w_down, n_fpost,
          w_ple, w_ple_gate, n_ple, seq):
    t, d = x2.shape
    d_ff = w_down.shape[0]
    tm = min(TM_TAIL, seq)
    tok = lambda w: pl.BlockSpec((tm, w), lambda i: (i, 0))
    row = lambda v: v.reshape(1, -1)
    b16 = lambda w: w.astype(BF16)
    consts = [b16(w_a2d), b16(w_b2d), b16(w_o), row(n_mix), row(n_fpre), b16(w_up), conv_ffn, row(conv_b),
              b16(w_down), row(n_fpost), b16(w_ple), b16(w_ple_gate), row(n_ple)]
    return pl.pallas_call(
        functools.partial(_tail_kernel, tm=tm, seq=seq, d_ff=d_ff),
        grid=(t // tm,),
        in_specs=[tok(d), tok(ya.shape[1]), tok(yb.shape[1]), tok(2 * d), tok(p2.shape[1])]
                 + [_resident(c.shape) for c in consts],
        out_specs=tok(d),
        out_shape=jax.ShapeDtypeStruct((t, d), F32),
        scratch_shapes=[pltpu.VMEM((tm + SUBLANES, FC_TAIL), F32),
                        pltpu.VMEM((SUBLANES, 2 * d_ff), F32),
                        pltpu.VMEM((tm, d_ff), BF16)],
        compiler_params=pltpu.CompilerParams(dimension_semantics=("arbitrary",), vmem_limit_bytes=VMEM_LIMIT),
        name="tail_ffn",
    )(x2, ya, yb, gmix, p2, *consts)


def _layer(h2, p2, batch, seq, norm_mix_pre, w_in, conv_qkv, a_log, dt_bias, gdn_norm,
           cmp_pos_k, cmp_w1_k, cmp_w2_k, cmp_pos_v, cmp_w1_v, cmp_w2_v,
           w_a2d, w_b2d, w_o, norm_mix_post, norm_ffn_pre, w_up, conv_ffn, conv_ffn_b,
           w_down, norm_ffn_post, w_ple, w_ple_gate, norm_ple_post):
    qkv, z, qb, cmp, kv4, gmix, small_a, gate_b = _in_proj(h2, norm_mix_pre, _prep_w_in(w_in))
    ya = _gdn(qkv, z, small_a, conv_qkv, a_log, dt_bias, gdn_norm, batch, seq)
    kcv = _compress(cmp, cmp_pos_k, cmp_w1_k, cmp_w2_k, cmp_pos_v, cmp_w1_v, cmp_w2_v, batch, seq)
    yb = _nsa_attention(qb, kcv, kv4, gate_b, batch, seq)
    return _tail(h2, ya, yb, gmix, p2, w_a2d, w_b2d, w_o, norm_mix_post, norm_ffn_pre, w_up, conv_ffn,
                 conv_ffn_b, w_down, norm_ffn_post, w_ple, w_ple_gate, norm_ple_post, seq)


def kernel(x, p, norm_mix_pre, w_in, conv_qkv, a_log, dt_bias, gdn_norm, cmp_pos_k, cmp_w1_k, cmp_w2_k, cmp_pos_v, cmp_w1_v, cmp_w2_v, w_a2d, w_b2d, w_o, norm_mix_post, norm_ffn_pre, w_up, conv_ffn, conv_ffn_b, w_down, norm_ffn_post, w_ple, w_ple_gate, norm_ple_post):
    batch, seq, d = x.shape
    params = (norm_mix_pre, w_in, conv_qkv, a_log, dt_bias, gdn_norm, cmp_pos_k, cmp_w1_k, cmp_w2_k,
              cmp_pos_v, cmp_w1_v, cmp_w2_v, w_a2d, w_b2d, w_o, norm_mix_post, norm_ffn_pre, w_up,
              conv_ffn, conv_ffn_b, w_down, norm_ffn_post, w_ple, w_ple_gate, norm_ple_post)
    h = x.reshape(batch * seq, d)
    for i in range(p.shape[0]):
        h = _layer(h, p[i].reshape(batch * seq, -1), batch, seq, *[w[i] for w in params])
    return h.reshape(batch, seq, d)
```

```python
import functools

import numpy as np
import jax
import jax.numpy as jnp
from jax import lax
from jax.experimental import pallas as pl
from jax.experimental.pallas import tpu as pltpu

F32 = jnp.float32
BF16 = jnp.bfloat16
EPS = 1e-6
NEG = -1e30
HIGHEST = lax.Precision.HIGHEST

GDN_HEADS, GDN_DK, GDN_DV, GDN_CONV, GDN_CHUNK = 4, 128, 128, 4, 64
NSA_HEADS, NSA_GROUPS, NSA_DK, NSA_DV = 8, 2, 64, 64
NSA_HPG = NSA_HEADS // NSA_GROUPS
CMP_LEN, CMP_STRIDE, CMP_HIDDEN = 32, 16, 256
SEL_BLOCK, SEL_TOPN = 64, 16
WINDOW = 512
FORCED_SCORE = 1e9
FFN_CONV = 3
IN_SPLITS = (512, 512, 512, 512, 4, 4, 512, 128, 128, 128, 128, 128, 128, 24, 2048)

LANES = 128
SUBLANES = 8
VMEM_LIMIT = 56 * 1024 * 1024

TM_PROJ = 512
TC_GDN = 256
TQ_NSA = 256
TM_TAIL = 256
FC_TAIL = 256


def _log2(n):
    assert n & (n - 1) == 0
    return n.bit_length() - 1


def _dot(a, b, precision=None):
    return jnp.dot(a, b, preferred_element_type=F32, precision=precision)


def _dot_nt(a, b):
    return lax.dot_general(a, b, (((1,), (1,)), ((), ())), preferred_element_type=F32)


def _rms(x, w):
    return x * lax.rsqrt(jnp.mean(x * x, axis=-1, keepdims=True) + EPS) * w


def _sigmoid(x):
    return 1.0 / (1.0 + jnp.exp(-x))


def _silu(x):
    return x * _sigmoid(x)


def _gelu_tanh(x):
    c = 0.7978845608028654
    half = 0.5 * x
    return half + half * jnp.tanh(x * (c + (c * 0.044715) * (x * x)))


def _softplus(x):
    return jnp.maximum(x, 0.0) + jnp.log1p(jnp.exp(-jnp.abs(x)))


def _resident(shape):
    zeros = (0,) * len(shape)
    return pl.BlockSpec(shape, lambda *_: zeros, pipeline_mode=pl.Buffered(1))


PROJ_WIDTHS = (1536, 512, 512, 256, 512, 2048, 128, 256)
PROJ_DTYPES = (F32, F32, BF16, F32, BF16, F32, F32, F32)
PROJ_COLS = 512


def _in_proj_kernel(x_ref, nw_ref, w_ref, *out_refs):
    x = x_ref[...]
    u = _rms(x, nw_ref[...]).astype(BF16)
    off = 0
    for o_ref, width in zip(out_refs, PROJ_WIDTHS):
        for c0 in range(0, width, PROJ_COLS):
            c1 = min(c0 + PROJ_COLS, width)
            o_ref[:, c0:c1] = _dot(u, w_ref[:, off + c0:off + c1]).astype(o_ref.dtype)
        off += width


def _prep_w_in(w_in):
    offs = np.cumsum((0,) + IN_SPLITS)
    qa, ka, va, za, ba, aa, qb, kcm, vcm, ksl, vsl, kwi, vwi, gnsa, gmix = [
        w_in[:, offs[i]:offs[i + 1]] for i in range(len(IN_SPLITS))]
    d = w_in.shape[0]
    zeros = lambda n: jnp.zeros((d, n), w_in.dtype)
    qb = qb * (NSA_DK ** -0.5)
    grp = lambda y, g: y[:, g * NSA_DK:(g + 1) * NSA_DK]
    kv4 = jnp.concatenate([grp(ksl, 0), grp(vsl, 0), grp(ksl, 1), grp(vsl, 1),
                           grp(kwi, 0), grp(vwi, 0), grp(kwi, 1), grp(vwi, 1)], axis=1)
    small_a = jnp.concatenate([ba, aa, zeros(LANES - 2 * GDN_HEADS)], axis=1)
    ng = NSA_HPG * 3
    gate_b = jnp.concatenate([gnsa[:, :ng], zeros(LANES - ng), gnsa[:, ng:], zeros(LANES - ng)], axis=1)
    w = jnp.concatenate([qa, ka, va, za, qb, kcm, vcm, kv4, gmix, small_a, gate_b], axis=1)
    assert w.shape[1] == sum(PROJ_WIDTHS)
    return w.astype(BF16)


def _in_proj(x2, norm_w, w_prep):
    t, d = x2.shape
    tm = min(TM_PROJ, t)
    n = w_prep.shape[1]
    return pl.pallas_call(
        _in_proj_kernel,
        grid=(t // tm,),
        in_specs=[pl.BlockSpec((tm, d), lambda i: (i, 0)),
                  _resident((1, d)),
                  _resident((d, n))],
        out_specs=[pl.BlockSpec((tm, wd), lambda i: (i, 0)) for wd in PROJ_WIDTHS],
        out_shape=[jax.ShapeDtypeStruct((t, wd), dt) for wd, dt in zip(PROJ_WIDTHS, PROJ_DTYPES)],
        compiler_params=pltpu.CompilerParams(dimension_semantics=("arbitrary",), vmem_limit_bytes=VMEM_LIMIT),
        name="in_proj",
    )(x2, norm_w.reshape(1, d), w_prep)


def _gdn_kernel(qkv_ref, z_ref, sm_ref, cw_ref, alog_ref, dtb_ref, nw_ref, o_ref,
                xbuf, qn, kn, vn, gcs, bts, state, *, tc):
    H, DK, C = GDN_HEADS, GDN_DK, GDN_CHUNK
    j = pl.program_id(1)

    @pl.when(j == 0)
    def _():
        xbuf[0:SUBLANES, :] = jnp.zeros((SUBLANES, xbuf.shape[1]), F32)
        state[...] = jnp.zeros(state.shape, F32)

    xbuf[SUBLANES:SUBLANES + tc, :] = qkv_ref[...]

    for blk in range(3 * H):
        cs = slice(blk * DK, (blk + 1) * DK)
        conv = cw_ref[GDN_CONV - 1:GDN_CONV, cs] * xbuf[SUBLANES:SUBLANES + tc, cs]
        for kk in range(GDN_CONV - 1):
            r0 = SUBLANES - (GDN_CONV - 1) + kk
            conv = conv + cw_ref[kk:kk + 1, cs] * xbuf[r0:r0 + tc, cs]
        act = _silu(conv)
        which, h = divmod(blk, H)
        if which == 0:
            qn[h] = act * lax.rsqrt(jnp.sum(act * act, axis=-1, keepdims=True) + EPS) * (DK ** -0.5)
        elif which == 1:
            kn[h] = act * lax.rsqrt(jnp.sum(act * act, axis=-1, keepdims=True) + EPS)
        else:
            vn[h] = act
    xbuf[0:SUBLANES, :] = xbuf[tc:tc + SUBLANES, :]

    sm = sm_ref[...]
    bts[...] = _sigmoid(sm)
    g_all = -jnp.exp(alog_ref[...]) * _softplus(sm + dtb_ref[...])
    ri = lax.broadcasted_iota(jnp.int32, (tc, tc), 0)
    ci = lax.broadcasted_iota(jnp.int32, (tc, tc), 1)
    same_chunk = (ri >> _log2(C)) == (ci >> _log2(C))
    block_tril = jnp.where(same_chunk & (ri >= ci), 1.0, 0.0).astype(F32)
    gcs[...] = _dot(block_tril, g_all, precision=HIGHEST)

    i64 = lax.broadcasted_iota(jnp.int32, (C, C), 0)
    j64 = lax.broadcasted_iota(jnp.int32, (C, C), 1)
    incl = i64 >= j64
    strict = i64 > j64
    eye = jnp.where(i64 == j64, 1.0, 0.0).astype(F32)
    pair_masks = [((i64 >> (lv + 1)) == (j64 >> (lv + 1))) & ((i64 >> lv) != (j64 >> lv)) for lv in range(_log2(C))]
    nw = nw_ref[...]

    units = [(c, h) for c in range(tc // C) for h in range(H)]
    rows = [slice(c * C, (c + 1) * C) for c in range(tc // C)]
    gc = [gcs[r, :] for r in rows]
    bt = [bts[r, :] for r in rows]
    gc_t = [x.T for x in gc]
    bt_t = [x.T for x in bt]
    gc_col = [gc[c][:, H + h:H + h + 1] for c, h in units]
    g_last = [gc[c][C - 1:C, H + h:H + h + 1] for c, h in units]
    e = [jnp.exp(jnp.where(incl, gc_col[n] - gc_t[c][H + h:H + h + 1, :], 0.0)) for n, (c, h) in enumerate(units)]
    q = [qn[h, rows[c], :] for c, h in units]
    k = [kn[h, rows[c], :] for c, h in units]
    kb = [x.astype(BF16) for x in k]
    kkt = [_dot_nt(x, x) for x in kb]
    qkm = [(_dot_nt(q[n].astype(BF16), kb[n]) * jnp.where(incl, e[n], 0.0)).astype(BF16) for n in range(len(units))]
    low = [bt[c][:, h:h + 1] * kkt[n] * jnp.where(strict, e[n], 0.0) for n, (c, h) in enumerate(units)]
    acc = [-jnp.where(pair_masks[0], x, 0.0) for x in low]
    for pm in pair_masks[1:]:
        off = [jnp.where(pm, x, 0.0) for x in low]
        tb = [off[n] + _dot(acc[n].astype(BF16), off[n].astype(BF16)) for n in range(len(units))]
        acc = [acc[n] - tb[n] - _dot(tb[n].astype(BF16), acc[n].astype(BF16)) for n in range(len(units))]
    t_beta = [((eye + acc[n]) * bt_t[c][h:h + 1, :]).astype(BF16) for n, (c, h) in enumerate(units)]
    gam = [jnp.exp(x) for x in gc_col]
    sol = [_dot(t_beta[n], jnp.concatenate([k[n] * gam[n], vn[h, rows[c], :]], axis=1).astype(BF16))
           for n, (c, h) in enumerate(units)]
    wq = [jnp.concatenate([sol[n][:, :DK], q[n] * gam[n]], axis=0).astype(BF16) for n in range(len(units))]
    kt_t = [(k[n] * jnp.exp(g_last[n] - gc_col[n])).T.astype(BF16) for n in range(len(units))]
    s_decay = [jnp.exp(x) for x in g_last]

    s_cur = [state[h] for h in range(H)]
    for n, (c, h) in enumerate(units):
        ws = _dot(wq[n], s_cur[h].astype(BF16))
        ub = (sol[n][:, DK:] - ws[:C]).astype(BF16)
        o = ws[C:] + _dot(qkm[n], ub)
        s_cur[h] = s_decay[n] * s_cur[h] + _dot(kt_t[n], ub)
        y = _rms(o, nw) * _silu(z_ref[rows[c], h * GDN_DV:(h + 1) * GDN_DV])
        o_ref[rows[c], h * GDN_DV:(h + 1) * GDN_DV] = y.astype(o_ref.dtype)
    for h in range(H):
        state[h] = s_cur[h]


def _gdn(qkv, z, small_a, conv_w, a_log, dt_bias, norm_w, batch, seq):
    t = qkv.shape[0]
    tc = min(TC_GDN, seq)
    nj = seq // tc
    H = GDN_HEADS
    pad = lambda v: jnp.zeros((1, LANES), F32).at[0, H:2 * H].set(v.astype(F32))
    tok = lambda w: pl.BlockSpec((tc, w), lambda b, j: (b * nj + j, 0))
    return pl.pallas_call(
        functools.partial(_gdn_kernel, tc=tc),
        grid=(batch, nj),
        in_specs=[tok(3 * H * GDN_DK), tok(H * GDN_DV), tok(LANES),
                  _resident(conv_w.shape), _resident((1, LANES)), _resident((1, LANES)),
                  _resident((1, GDN_DV))],
        out_specs=tok(H * GDN_DV),
        out_shape=jax.ShapeDtypeStruct((t, H * GDN_DV), BF16),
        scratch_shapes=[pltpu.VMEM((tc + SUBLANES, 3 * H * GDN_DK), F32),
                        pltpu.VMEM((H, tc, GDN_DK), F32),
                        pltpu.VMEM((H, tc, GDN_DK), F32),
                        pltpu.VMEM((H, tc, GDN_DV), F32),
                        pltpu.VMEM((tc, LANES), F32),
                        pltpu.VMEM((tc, LANES), F32),
                        pltpu.VMEM((H, GDN_DK, GDN_DV), F32)],
        compiler_params=pltpu.CompilerParams(dimension_semantics=("arbitrary", "arbitrary"),
                                             vmem_limit_bytes=VMEM_LIMIT),
        name="gdn_mixer",
    )(qkv, z, small_a, conv_w, pad(a_log), pad(dt_bias), norm_w.reshape(1, GDN_DV))


def _compress_kernel(x_ref, pos_ref, w1_ref, w2_ref, o_ref, shift, *, nsub):
    half = CMP_STRIDE * NSA_DK
    x = x_ref[0, 0]
    pos = pos_ref[0]
    first = _dot((x + pos[:, :half]).astype(BF16), w1_ref[0, :half, :])
    second = _dot((x + pos[:, half:]).astype(BF16), w1_ref[0, half:, :])
    shift[0:nsub, :] = second
    shift[nsub:nsub + SUBLANES, :] = jnp.zeros((SUBLANES, CMP_HIDDEN), F32)
    hidden = first + shift[1:nsub + 1, :]
    y = _dot(_gelu_tanh(hidden).astype(BF16), w2_ref[0])
    row = lax.broadcasted_iota(jnp.int32, y.shape, 0)
    o_ref[0, 0] = jnp.where(row < nsub - 1, y, 0.0)


def _compress(cmp, pos_k, w1_k, w2_k, pos_v, w1_v, w2_v, batch, seq):
    G, d = NSA_GROUPS, NSA_DK
    nsub = seq // CMP_STRIDE
    xs = cmp.reshape(batch, nsub, CMP_STRIDE, 2 * G, d).transpose(0, 3, 1, 2, 4).reshape(batch, 2 * G, nsub, CMP_STRIDE * d)
    pos = jnp.stack([pos_k.reshape(1, -1), pos_v.reshape(1, -1)])
    w1 = jnp.stack([w1_k, w1_v]).astype(BF16)
    w2 = jnp.stack([w2_k, w2_v]).astype(BF16)
    out = pl.pallas_call(
        functools.partial(_compress_kernel, nsub=nsub),
        grid=(batch, 2 * G),
        in_specs=[pl.BlockSpec((1, 1, nsub, CMP_STRIDE * d), lambda b, c: (b, c, 0, 0)),
                  pl.BlockSpec((1, 1, CMP_LEN * d), lambda b, c: (c // G, 0, 0)),
                  pl.BlockSpec((1, CMP_LEN * d, CMP_HIDDEN), lambda b, c: (c // G, 0, 0)),
                  pl.BlockSpec((1, CMP_HIDDEN, d), lambda b, c: (c // G, 0, 0))],
        out_specs=pl.BlockSpec((1, 1, nsub, d), lambda b, c: (b, c, 0, 0)),
        out_shape=jax.ShapeDtypeStruct((batch, 2 * G, nsub, d), F32),
        scratch_shapes=[pltpu.VMEM((nsub + SUBLANES, CMP_HIDDEN), F32)],
        compiler_params=pltpu.CompilerParams(dimension_semantics=("arbitrary", "arbitrary"),
                                             vmem_limit_bytes=VMEM_LIMIT),
        name="nsa_compress",
    )(xs, pos, w1, w2)
    return jnp.concatenate([out[:, :G], out[:, G:]], axis=-1).astype(BF16)


NSA_MASK = -1e30
NSA_M0 = -1e29
NSA_VROWS = NSA_DV + 16
SEL, WIN = 0, 1
NSA_ONEHOT_ROW = 80
NSA_BIAS_ROWS = 16


def _nsa_kernel(q_ref, kcv_ref, ksel_ref, kwin_ref, gate_ref, slope_ref, o_ref,
                kc_aug, vc_t, k_all, v_all, q_sc, sbuf, negq, m_all, acc_all, *, tq, seq):
    R, DK, DV = NSA_HPG, NSA_DK, NSA_DV
    nsub = seq // CMP_STRIDE
    nsel = seq // SEL_BLOCK
    topn = min(SEL_TOPN, nsel)
    nql = R * tq
    per_step = tq // SEL_BLOCK
    nstep_sel = nsel // per_step
    pad = WINDOW // tq
    i = pl.program_id(2)
    t0 = i * tq
    d_kq = (lax.broadcasted_iota(jnp.int32, (tq, nql), 0)
            - (lax.broadcasted_iota(jnp.int32, (tq, nql), 1) & (tq - 1)))

    def aug_keys(kv, pos):
        lane = lax.broadcasted_iota(jnp.int32, kv.shape, 1)
        block_in_step = (pos >> _log2(SEL_BLOCK)) & (per_step - 1)
        onehot = jnp.where(lane - NSA_ONEHOT_ROW == block_in_step, 1.0, 0.0)
        aug = jnp.where(lane == DK, (pos >> 7).astype(F32),
                        jnp.where(lane == DK + 1, (pos & 127).astype(F32), jnp.where(lane == DK + 2, 1.0, onehot)))
        return jnp.where(lane < DK, kv, aug).astype(BF16)

    def values_t(kv):
        t = kv.T
        return jnp.concatenate([t[DK:DK + DV], jnp.ones((NSA_VROWS - DV, kv.shape[0]), F32)], axis=0).astype(BF16)

    @pl.when(i == 0)
    def _():
        lane = lax.broadcasted_iota(jnp.int32, (pad * tq, LANES), 1)
        pad_keys = jnp.where(lane == DK, NSA_MASK, 0.0).astype(BF16)
        for br, src in ((SEL, ksel_ref), (WIN, kwin_ref)):
            k_all[br, 0:pad * tq, :] = pad_keys
            v_all[br, 0:pad] = jnp.zeros((pad, NSA_VROWS, tq), BF16)

        def fill(c, carry):
            r0 = pl.multiple_of(c * tq, tq)
            pos = lax.broadcasted_iota(jnp.int32, (tq, LANES), 0) + r0
            for br, src in ((SEL, ksel_ref), (WIN, kwin_ref)):
                kv = src[pl.ds(r0, tq), :].astype(F32)
                k_all[br, pl.ds(r0 + pad * tq, tq), :] = aug_keys(kv, pos)
                v_all[br, c + pad] = values_t(kv)
            return carry

        lax.fori_loop(0, seq // tq, fill, 0)
        kcv = kcv_ref[0, 0].astype(F32)
        pos_c = lax.broadcasted_iota(jnp.int32, (nsub, LANES), 0) * CMP_STRIDE + (CMP_LEN - 1)
        kc_aug[...] = aug_keys(kcv, pos_c)
        vc_t[...] = kcv.T[DK:DK + DV].astype(BF16)
        negq[nstep_sel] = jnp.zeros((NSA_BIAS_ROWS, nql), BF16)

    row_q = lax.broadcasted_iota(jnp.int32, (LANES - DK, tq), 0)
    t0_f = (jnp.zeros((LANES - DK, tq), jnp.int32) + t0).astype(F32)
    q_all = q_ref[...].astype(F32).T
    qts = []
    for r in range(R):
        sl = slope_ref[0, r:r + 1, 0:1]
        aug = jnp.where(row_q == 0, sl * 128.0, jnp.where(row_q == 1, sl, jnp.where(row_q == 2, -sl * t0_f, 0.0)))
        qts.append(jnp.concatenate([q_all[r * DK:(r + 1) * DK], aug], axis=0))
    q_t = jnp.concatenate(qts, axis=1).astype(BF16)

    blk_end = lax.broadcasted_iota(jnp.int32, (nsub, nql), 0) * CMP_STRIDE + (CMP_LEN - 1)
    q_pos = (lax.broadcasted_iota(jnp.int32, (nsub, nql), 1) & (tq - 1)) + t0
    s = jnp.where(blk_end <= q_pos, _dot(kc_aug[...], q_t), NSA_MASK)
    m = jnp.maximum(jnp.max(s, axis=0, keepdims=True), NSA_M0)
    e = jnp.exp(s - m)
    den = jnp.sum(e, axis=0, keepdims=True)
    p = e * jnp.where(den > 0.0, 1.0 / den, 0.0)
    o_cmp = _dot(vc_t[...], p.astype(BF16))
    psum = p[:, 0:tq]
    for r in range(1, R):
        psum = psum + p[:, r * tq:(r + 1) * tq]

    nwin = WINDOW // tq + 1
    oh = NSA_ONEHOT_ROW

    def keys_of(br, step):
        return k_all[br, pl.ds(pl.multiple_of(step * tq, tq), tq), :]

    def accumulate(s, br, step):
        m_old = m_all[br]
        m_new = jnp.maximum(m_old, jnp.max(s, axis=0, keepdims=True))
        p = jnp.exp(s - m_new).astype(BF16)
        acc_all[br] = jnp.exp(m_old - m_new) * acc_all[br] + _dot(v_all[br, step], p)
        m_all[br] = m_new

    m_all[...] = jnp.full(m_all.shape, NSA_M0, F32)
    acc_all[...] = jnp.zeros(acc_all.shape, F32)

    s_win = [_dot(keys_of(WIN, i + w), q_t) for w in range(nwin)]
    s_win[0] = jnp.where(d_kq > 0, s_win[0], NSA_MASK)
    s_win[-1] = jnp.where(d_kq <= 0, s_win[-1], NSA_MASK)
    m_win = jnp.max(s_win[0], axis=0, keepdims=True)
    for s_w in s_win[1:]:
        m_win = jnp.maximum(m_win, jnp.max(s_w, axis=0, keepdims=True))
    pv_win = []

    def window_piece(w):
        pv_win.append(_dot(v_all[WIN, i + w], jnp.exp(s_win[w] - m_win).astype(BF16)))

    j_o = lax.broadcasted_iota(jnp.int32, (nsel, nsub), 0) * SEL_BLOCK
    n_o = lax.broadcasted_iota(jnp.int32, (nsel, nsub), 1) * CMP_STRIDE
    ov = jnp.maximum(jnp.minimum(n_o + CMP_LEN, j_o + SEL_BLOCK) - jnp.maximum(n_o, j_o), 0)
    ov = (ov.astype(F32) * (1.0 / CMP_LEN)).astype(BF16)
    p_hi = psum.astype(BF16)
    p_lo = (psum - p_hi.astype(F32)).astype(BF16)
    imp = _dot(ov, p_hi) + _dot(ov, p_lo)
    blk = lax.broadcasted_iota(jnp.int32, (nsel, tq), 0)
    cur = (lax.broadcasted_iota(jnp.int32, (nsel, tq), 1) + t0) >> _log2(SEL_BLOCK)
    valid_s = blk <= cur
    forced = (blk == 0) | (blk == cur) | (blk == cur - 1)
    score = jnp.where(valid_s, jnp.where(forced, FORCED_SCORE, imp), -jnp.inf)
    blk_f = blk.astype(F32)
    npieces = nwin
    done = 0
    for it in range(topn):
        m = jnp.max(score, axis=0, keepdims=True)
        first = jnp.min(jnp.where(score == m, blk_f, float(nsel)), axis=0, keepdims=True)
        score = jnp.where(blk_f == first, -jnp.inf, score)
        while done < ((it + 1) * npieces) // topn:
            window_piece(done)
            done += 1
    acc_all[WIN] = functools.reduce(lambda a, b: a + b, pv_win)
    m_all[WIN] = m_win
    nb = jnp.where(valid_s & (score == -jnp.inf), 0.0, NSA_MASK)
    for n in range(nstep_sel):
        rows = jnp.concatenate([nb[n * per_step:(n + 1) * per_step],
                                jnp.zeros((NSA_BIAS_ROWS - per_step, tq), F32)], axis=0)
        negq[n] = jnp.concatenate([rows] * R, axis=1).astype(BF16)

    q_diag = jnp.concatenate([q_t[:oh], negq[i], q_t[oh + NSA_BIAS_ROWS:]], axis=0)
    accumulate(jnp.where(d_kq <= 0, _dot(keys_of(SEL, i + pad), q_diag), NSA_MASK), SEL, i + pad)

    def schedule(n):
        dummy = n >= i
        return jnp.where(dummy, WIN, SEL), jnp.where(dummy, 0, n + pad), jnp.where(dummy, nstep_sel, n)

    def issue_scores(n, slot):
        br, step, group = schedule(n)
        q_sc[oh:oh + NSA_BIAS_ROWS, :] = negq[group]
        sbuf[slot] = _dot(keys_of(br, step), q_sc[...])

    def consume(n, slot):
        br, step, _ = schedule(n)
        accumulate(sbuf[slot], br, step)

    q_sc[...] = q_t
    issue_scores(0, 0)

    def body(m, carry):
        n = 2 * m
        issue_scores(n + 1, 1)
        consume(n, 0)
        issue_scores(n + 2, 0)
        consume(n + 1, 1)
        return carry

    lax.fori_loop(0, (i + 1) >> 1, body, 0)
    acc_s = acc_all[SEL]
    acc_w = acc_all[WIN]
    o_sel = acc_s[:DV] / acc_s[DV:DV + 1]
    o_win = acc_w[:DV] / acc_w[DV:DV + 1]

    g_t = _sigmoid(gate_ref[...]).T
    ys = []
    for r in range(R):
        cs = slice(r * tq, (r + 1) * tq)
        ys.append(g_t[3 * r:3 * r + 1] * o_cmp[:, cs] + g_t[3 * r + 1:3 * r + 2] * o_sel[:, cs]
                  + g_t[3 * r + 2:3 * r + 3] * o_win[:, cs])
    for half in range(R // 2):
        pair = jnp.concatenate([ys[2 * half], ys[2 * half + 1]], axis=0)
        o_ref[:, half * LANES:(half + 1) * LANES] = pair.T.astype(o_ref.dtype)


def _nsa_attention(qb, kcv, kv4, gate_b, batch, seq):
    t = qb.shape[0]
    G, R = NSA_GROUPS, NSA_HPG
    tq = TQ_NSA
    assert tq % LANES == 0 and WINDOW % tq == 0 and seq % tq == 0 and LANES == 2 * NSA_DK
    nq = seq // tq
    nsub = seq // CMP_STRIDE
    nsel = seq // SEL_BLOCK
    slopes = 2.0 ** (-8.0 * jnp.arange(1, NSA_HEADS + 1, dtype=F32) / NSA_HEADS)
    slope_rows = jnp.zeros((G, SUBLANES, LANES), F32).at[:, :R, :].set(
        jnp.broadcast_to(slopes.reshape(G, R, 1), (G, R, LANES)))
    return pl.pallas_call(
        functools.partial(_nsa_kernel, tq=tq, seq=seq),
        grid=(batch, G, nq),
        in_specs=[pl.BlockSpec((tq, R * NSA_DK), lambda b, g, i: (b * nq + i, g)),
                  pl.BlockSpec((1, 1, nsub, LANES), lambda b, g, i: (b, g, 0, 0)),
                  pl.BlockSpec((seq, LANES), lambda b, g, i: (b, g)),
                  pl.BlockSpec((seq, LANES), lambda b, g, i: (b, G + g)),
                  pl.BlockSpec((tq, LANES), lambda b, g, i: (b * nq + i, g)),
                  pl.BlockSpec((1, SUBLANES, LANES), lambda b, g, i: (g, 0, 0))],
        out_specs=pl.BlockSpec((tq, R * NSA_DV), lambda b, g, i: (b * nq + i, g)),
        out_shape=jax.ShapeDtypeStruct((t, NSA_HEADS * NSA_DV), BF16),
        scratch_shapes=[pltpu.VMEM((nsub, LANES), BF16),
                        pltpu.VMEM((NSA_DV, nsub), BF16),
                        pltpu.VMEM((2, seq + WINDOW, LANES), BF16),
                        pltpu.VMEM((2, (seq + WINDOW) // tq, NSA_VROWS, tq), BF16),
                        pltpu.VMEM((LANES, R * tq), BF16),
                        pltpu.VMEM((2, tq, R * tq), F32),
                        pltpu.VMEM((nsel // (tq // SEL_BLOCK) + 1, NSA_BIAS_ROWS, R * tq), BF16),
                        pltpu.VMEM((2, 1, R * tq), F32),
                        pltpu.VMEM((2, NSA_VROWS, R * tq), F32)],
        compiler_params=pltpu.CompilerParams(dimension_semantics=("arbitrary", "arbitrary", "arbitrary"),
                                             vmem_limit_bytes=VMEM_LIMIT),
        name="nsa_attention",
    )(qb, kcv, kv4, kv4, gate_b, slope_rows)


def _tail_kernel(x_ref, ya_ref, yb_ref, gm_ref, p_ref, wa_ref, wb_ref, wo_ref, nmix_ref, nfpre_ref,
                 wup_ref, cw_ref, cb_ref, wdn_ref, nfpost_ref, wple_ref, wpg_ref, nple_ref, o_ref,
                 ext, tail, act, *, tm, seq, d_ff):
    d = x_ref.shape[1]
    halo = FFN_CONV - 1

    @pl.when((pl.program_id(0) * tm) % seq == 0)
    def _():
        tail[...] = jnp.zeros(tail.shape, F32)

    gm = gm_ref[...]
    mixed = _sigmoid(gm[:, :d]) * _dot(ya_ref[...], wa_ref[...]) + _sigmoid(gm[:, d:]) * _dot(yb_ref[...], wb_ref[...])
    h1 = x_ref[...] + _rms(_dot(mixed.astype(BF16), wo_ref[...]), nmix_ref[...])

    u = _rms(h1, nfpre_ref[...]).astype(BF16)
    for c in range(d_ff // FC_TAIL):
        fs = []
        for part in range(2):
            cols = slice(part * d_ff + c * FC_TAIL, part * d_ff + (c + 1) * FC_TAIL)
            ext[0:SUBLANES, :] = tail[:, cols]
            ext[SUBLANES:SUBLANES + tm, :] = _dot(u, wup_ref[:, cols])
            f = cb_ref[:, cols] + cw_ref[halo:halo + 1, cols] * ext[SUBLANES:SUBLANES + tm, :]
            for k in range(halo):
                r0 = SUBLANES - halo + k
                f = f + cw_ref[k:k + 1, cols] * ext[r0:r0 + tm, :]
            tail[:, cols] = ext[tm:tm + SUBLANES, :]
            fs.append(f)
        act[:, c * FC_TAIL:(c + 1) * FC_TAIL] = (_gelu_tanh(fs[0]) * fs[1]).astype(BF16)
    h2 = h1 + _rms(_dot(act[...], wdn_ref[...]), nfpost_ref[...])

    e = _dot(p_ref[...].astype(BF16), wple_ref[...]) * _sigmoid(_dot(h2.astype(BF16), wpg_ref[...]))
    o_ref[...] = h2 + _rms(e, nple_ref[...])


def _tail(x2, ya, yb, gmix, p2, w_a2d, w_b2d, w_o, n_mix, n_fpre, w_up, conv_ffn, conv_b, w_down, n_fpost,
          w_ple, w_ple_gate, n_ple, seq):
    t, d = x2.shape
    d_ff = w_down.shape[0]
    tm = min(TM_TAIL, seq)
    tok = lambda w: pl.BlockSpec((tm, w), lambda i: (i, 0))
    row = lambda v: v.reshape(1, -1)
    b16 = lambda w: w.astype(BF16)
    consts = [b16(w_a2d), b16(w_b2d), b16(w_o), row(n_mix), row(n_fpre), b16(w_up), conv_ffn, row(conv_b),
              b16(w_down), row(n_fpost), b16(w_ple), b16(w_ple_gate), row(n_ple)]
    return pl.pallas_call(
        functools.partial(_tail_kernel, tm=tm, seq=seq, d_ff=d_ff),
        grid=(t // tm,),
        in_specs=[tok(d), tok(ya.shape[1]), tok(yb.shape[1]), tok(2 * d), tok(p2.shape[1])]
                 + [_resident(c.shape) for c in consts],
        out_specs=tok(d),
        out_shape=jax.ShapeDtypeStruct((t, d), F32),
        scratch_shapes=[pltpu.VMEM((tm + SUBLANES, FC_TAIL), F32),
                        pltpu.VMEM((SUBLANES, 2 * d_ff), F32),
                        pltpu.VMEM((tm, d_ff), BF16)],
        compiler_params=pltpu.CompilerParams(dimension_semantics=("arbitrary",), vmem_limit_bytes=VMEM_LIMIT),
        name="tail_ffn",
    )(x2, ya, yb, gmix, p2, *consts)


def _layer(h2, p2, batch, seq, norm_mix_pre, w_in, conv_qkv, a_log, dt_bias, gdn_norm,
           cmp_pos_k, cmp_w1_k, cmp_w2_k, cmp_pos_v, cmp_w1_v, cmp_w2_v,
           w_a2d, w_b2d, w_o, norm_mix_post, norm_ffn_pre, w_up, conv_ffn, conv_ffn_b,
           w_down, norm_ffn_post, w_ple, w_ple_gate, norm_ple_post):
    qkv, z, qb, cmp, kv4, gmix, small_a, gate_b = _in_proj(h2, norm_mix_pre, _prep_w_in(w_in))
    ya = _gdn(qkv, z, small_a, conv_qkv, a_log, dt_bias, gdn_norm, batch, seq)
    kcv = _compress(cmp, cmp_pos_k, cmp_w1_k, cmp_w2_k, cmp_pos_v, cmp_w1_v, cmp_w2_v, batch, seq)
    yb = _nsa_attention(qb, kcv, kv4, gate_b, batch, seq)
    return _tail(h2, ya, yb, gmix, p2, w_a2d, w_b2d, w_o, norm_mix_post, norm_ffn_pre, w_up, conv_ffn,
                 conv_ffn_b, w_down, norm_ffn_post, w_ple, w_ple_gate, norm_ple_post, seq)


def kernel(x, p, norm_mix_pre, w_in, conv_qkv, a_log, dt_bias, gdn_norm, cmp_pos_k, cmp_w1_k, cmp_w2_k, cmp_pos_v, cmp_w1_v, cmp_w2_v, w_a2d, w_b2d, w_o, norm_mix_post, norm_ffn_pre, w_up, conv_ffn, conv_ffn_b, w_down, norm_ffn_post, w_ple, w_ple_gate, norm_ple_post):
    batch, seq, d = x.shape
    params = (norm_mix_pre, w_in, conv_qkv, a_log, dt_bias, gdn_norm, cmp_pos_k, cmp_w1_k, cmp_w2_k,
              cmp_pos_v, cmp_w1_v, cmp_w2_v, w_a2d, w_b2d, w_o, norm_mix_post, norm_ffn_pre, w_up,
              conv_ffn, conv_ffn_b, w_down, norm_ffn_post, w_ple, w_ple_gate, norm_ple_post)
    h = x.reshape(batch * seq, d)
    for i in range(p.shape[0]):
        h = _layer(h, p[i].reshape(batch * seq, -1), batch, seq, *[w[i] for w in params])
    return h.reshape(batch, seq, d)
```

```python
import functools

import numpy as np
import jax
import jax.numpy as jnp
from jax import lax
from jax.experimental import pallas as pl
from jax.experimental.pallas import tpu as pltpu

F32 = jnp.float32
BF16 = jnp.bfloat16
EPS = 1e-6
NEG = -1e30
HIGHEST = lax.Precision.HIGHEST

GDN_HEADS, GDN_DK, GDN_DV, GDN_CONV, GDN_CHUNK = 4, 128, 128, 4, 64
NSA_HEADS, NSA_GROUPS, NSA_DK, NSA_DV = 8, 2, 64, 64
NSA_HPG = NSA_HEADS // NSA_GROUPS
CMP_LEN, CMP_STRIDE, CMP_HIDDEN = 32, 16, 256
SEL_BLOCK, SEL_TOPN = 64, 16
WINDOW = 512
FORCED_SCORE = 1e9
FFN_CONV = 3
IN_SPLITS = (512, 512, 512, 512, 4, 4, 512, 128, 128, 128, 128, 128, 128, 24, 2048)

LANES = 128
SUBLANES = 8
VMEM_LIMIT = 56 * 1024 * 1024

TM_PROJ = 512
TC_GDN = 256
TQ_NSA = 256
TM_TAIL = 512
NSUB_TAIL = 2
FC_TAIL = 256


def _log2(n):
    assert n & (n - 1) == 0
    return n.bit_length() - 1


def _dot(a, b, precision=None):
    return jnp.dot(a, b, preferred_element_type=F32, precision=precision)


def _dot_nt(a, b):
    return lax.dot_general(a, b, (((1,), (1,)), ((), ())), preferred_element_type=F32)


def _rms(x, w):
    return x * lax.rsqrt(jnp.mean(x * x, axis=-1, keepdims=True) + EPS) * w


def _sigmoid(x):
    return 1.0 / (1.0 + jnp.exp(-x))


def _silu(x):
    return x * _sigmoid(x)


def _gelu_tanh(x):
    c = 0.7978845608028654
    half = 0.5 * x
    return half + half * jnp.tanh(x * (c + (c * 0.044715) * (x * x)))


def _softplus(x):
    return jnp.maximum(x, 0.0) + jnp.log1p(jnp.exp(-jnp.abs(x)))


def _resident(shape):
    zeros = (0,) * len(shape)
    return pl.BlockSpec(shape, lambda *_: zeros, pipeline_mode=pl.Buffered(1))


PROJ_WIDTHS = (1536, 512, 512, 256, 512, 2048, 128, 256)
PROJ_DTYPES = (F32, F32, BF16, F32, BF16, BF16, F32, F32)
PROJ_GATE = (False, False, False, False, False, True, False, False)
PROJ_COLS = 512


def _in_proj_kernel(x_ref, nw_ref, w_ref, *out_refs):
    x = x_ref[...]
    u = _rms(x, nw_ref[...]).astype(BF16)
    off = 0
    for o_ref, width, gate in zip(out_refs, PROJ_WIDTHS, PROJ_GATE):
        for c0 in range(0, width, PROJ_COLS):
            c1 = min(c0 + PROJ_COLS, width)
            y = _dot(u, w_ref[:, off + c0:off + c1])
            o_ref[:, c0:c1] = (_sigmoid(y) if gate else y).astype(o_ref.dtype)
        off += width


def _prep_w_in(w_in):
    offs = np.cumsum((0,) + IN_SPLITS)
    qa, ka, va, za, ba, aa, qb, kcm, vcm, ksl, vsl, kwi, vwi, gnsa, gmix = [
        w_in[:, offs[i]:offs[i + 1]] for i in range(len(IN_SPLITS))]
    d = w_in.shape[0]
    zeros = lambda n: jnp.zeros((d, n), w_in.dtype)
    qb = qb * (NSA_DK ** -0.5)
    grp = lambda y, g: y[:, g * NSA_DK:(g + 1) * NSA_DK]
    kv4 = jnp.concatenate([grp(ksl, 0), grp(vsl, 0), grp(ksl, 1), grp(vsl, 1),
                           grp(kwi, 0), grp(vwi, 0), grp(kwi, 1), grp(vwi, 1)], axis=1)
    small_a = jnp.concatenate([ba, aa, zeros(LANES - 2 * GDN_HEADS)], axis=1)
    ng = NSA_HPG * 3
    gate_b = jnp.concatenate([gnsa[:, :ng], zeros(LANES - ng), gnsa[:, ng:], zeros(LANES - ng)], axis=1)
    w = jnp.concatenate([qa, ka, va, za, qb, kcm, vcm, kv4, gmix, small_a, gate_b], axis=1)
    assert w.shape[1] == sum(PROJ_WIDTHS)
    return w.astype(BF16)


def _in_proj(x2, norm_w, w_prep):
    t, d = x2.shape
    tm = min(TM_PROJ, t)
    n = w_prep.shape[1]
    return pl.pallas_call(
        _in_proj_kernel,
        grid=(t // tm,),
        in_specs=[pl.BlockSpec((tm, d), lambda i: (i, 0)),
                  _resident((1, d)),
                  _resident((d, n))],
        out_specs=[pl.BlockSpec((tm, wd), lambda i: (i, 0)) for wd in PROJ_WIDTHS],
        out_shape=[jax.ShapeDtypeStruct((t, wd), dt) for wd, dt in zip(PROJ_WIDTHS, PROJ_DTYPES)],
        compiler_params=pltpu.CompilerParams(dimension_semantics=("arbitrary",), vmem_limit_bytes=VMEM_LIMIT),
        name="in_proj",
    )(x2, norm_w.reshape(1, d), w_prep)


def _gdn_kernel(qkv_ref, z_ref, sm_ref, cw_ref, alog_ref, dtb_ref, nw_ref, o_ref,
                xbuf, qn, kn, vn, gcs, bts, state, *, tc):
    H, DK, C = GDN_HEADS, GDN_DK, GDN_CHUNK

    @pl.when(pl.program_id(1) == 0)
    def _():
        xbuf[0:SUBLANES, :] = jnp.zeros((SUBLANES, xbuf.shape[1]), F32)
        state[...] = jnp.zeros(state.shape, F32)

    xbuf[SUBLANES:SUBLANES + tc, :] = qkv_ref[...]

    for blk in range(3 * H):
        cs = slice(blk * DK, (blk + 1) * DK)
        conv = cw_ref[GDN_CONV - 1:GDN_CONV, cs] * xbuf[SUBLANES:SUBLANES + tc, cs]
        for kk in range(GDN_CONV - 1):
            r0 = SUBLANES - (GDN_CONV - 1) + kk
            conv = conv + cw_ref[kk:kk + 1, cs] * xbuf[r0:r0 + tc, cs]
        act = _silu(conv)
        which, h = divmod(blk, H)
        if which == 0:
            qn[h] = act * lax.rsqrt(jnp.sum(act * act, axis=-1, keepdims=True) + EPS) * (DK ** -0.5)
        elif which == 1:
            kn[h] = act * lax.rsqrt(jnp.sum(act * act, axis=-1, keepdims=True) + EPS)
        else:
            vn[h] = act
    xbuf[0:SUBLANES, :] = xbuf[tc:tc + SUBLANES, :]

    sm = sm_ref[...]
    bts[...] = _sigmoid(sm)
    g_all = -jnp.exp(alog_ref[...]) * _softplus(sm + dtb_ref[...])
    ri = lax.broadcasted_iota(jnp.int32, (tc, tc), 0)
    ci = lax.broadcasted_iota(jnp.int32, (tc, tc), 1)
    same_chunk = (ri >> _log2(C)) == (ci >> _log2(C))
    block_tril = jnp.where(same_chunk & (ri >= ci), 1.0, 0.0).astype(F32)
    gcs[...] = _dot(block_tril, g_all, precision=HIGHEST)

    i64 = lax.broadcasted_iota(jnp.int32, (C, C), 0)
    j64 = lax.broadcasted_iota(jnp.int32, (C, C), 1)
    incl = i64 >= j64
    strict = i64 > j64
    eye = jnp.where(i64 == j64, 1.0, 0.0).astype(F32)
    pair_masks = [((i64 >> (lv + 1)) == (j64 >> (lv + 1))) & ((i64 >> lv) != (j64 >> lv)) for lv in range(_log2(C))]
    nw = nw_ref[...]

    units = [(c, h) for c in range(tc // C) for h in range(H)]
    rows = [slice(c * C, (c + 1) * C) for c in range(tc // C)]
    gc = [gcs[r, :] for r in rows]
    bt = [bts[r, :] for r in rows]
    gc_t = [x.T for x in gc]
    bt_t = [x.T for x in bt]
    gc_col = [gc[c][:, H + h:H + h + 1] for c, h in units]
    g_last = [gc[c][C - 1:C, H + h:H + h + 1] for c, h in units]
    e = [jnp.exp(jnp.where(incl, gc_col[n] - gc_t[c][H + h:H + h + 1, :], 0.0)) for n, (c, h) in enumerate(units)]
    q = [qn[h, rows[c], :] for c, h in units]
    k = [kn[h, rows[c], :] for c, h in units]
    kb = [x.astype(BF16) for x in k]
    kkt = [_dot_nt(x, x) for x in kb]
    qkm = [(_dot_nt(q[n].astype(BF16), kb[n]) * jnp.where(incl, e[n], 0.0)).astype(BF16) for n in range(len(units))]
    low = [bt[c][:, h:h + 1] * kkt[n] * jnp.where(strict, e[n], 0.0) for n, (c, h) in enumerate(units)]
    acc = [-jnp.where(pair_masks[0], x, 0.0) for x in low]
    for pm in pair_masks[1:]:
        off = [jnp.where(pm, x, 0.0) for x in low]
        tb = [off[n] + _dot(acc[n].astype(BF16), off[n].astype(BF16)) for n in range(len(units))]
        acc = [acc[n] - tb[n] - _dot(tb[n].astype(BF16), acc[n].astype(BF16)) for n in range(len(units))]
    t_beta = [((eye + acc[n]) * bt_t[c][h:h + 1, :]).astype(BF16) for n, (c, h) in enumerate(units)]
    gam = [jnp.exp(x) for x in gc_col]
    sol = [_dot(t_beta[n], jnp.concatenate([k[n] * gam[n], vn[h, rows[c], :]], axis=1).astype(BF16))
           for n, (c, h) in enumerate(units)]
    wq = [jnp.concatenate([sol[n][:, :DK], q[n] * gam[n]], axis=0).astype(BF16) for n in range(len(units))]
    kt_t = [(k[n] * jnp.exp(g_last[n] - gc_col[n])).T.astype(BF16) for n in range(len(units))]
    s_decay = [jnp.exp(x) for x in g_last]

    heads = range(H)
    s_cur = [state[h] for h in heads]
    for c in range(tc // C):
        ns = [c * H + h for h in heads]
        ws = [_dot(wq[n], s_cur[h].astype(BF16)) for h, n in zip(heads, ns)]
        ub = [(sol[n][:, DK:] - ws[h][:C]).astype(BF16) for h, n in zip(heads, ns)]
        o = [ws[h][C:] + _dot(qkm[n], ub[h]) for h, n in zip(heads, ns)]
        s_cur = [s_decay[n] * s_cur[h] + _dot(kt_t[n], ub[h]) for h, n in zip(heads, ns)]
        for h in heads:
            y = _rms(o[h], nw) * _silu(z_ref[rows[c], h * GDN_DV:(h + 1) * GDN_DV])
            o_ref[rows[c], h * GDN_DV:(h + 1) * GDN_DV] = y.astype(o_ref.dtype)
    for h in heads:
        state[h] = s_cur[h]


def _gdn(qkv, z, small_a, conv_w, a_log, dt_bias, norm_w, batch, seq):
    t = qkv.shape[0]
    tc = min(TC_GDN, seq)
    nj = seq // tc
    H = GDN_HEADS
    pad = lambda v: jnp.zeros((1, LANES), F32).at[0, H:2 * H].set(v.astype(F32))
    tok = lambda w: pl.BlockSpec((tc, w), lambda b, j: (b * nj + j, 0))
    return pl.pallas_call(
        functools.partial(_gdn_kernel, tc=tc),
        grid=(batch, nj),
        in_specs=[tok(3 * H * GDN_DK), tok(H * GDN_DV), tok(LANES),
                  _resident(conv_w.shape), _resident((1, LANES)), _resident((1, LANES)),
                  _resident((1, GDN_DV))],
        out_specs=tok(H * GDN_DV),
        out_shape=jax.ShapeDtypeStruct((t, H * GDN_DV), BF16),
        scratch_shapes=[pltpu.VMEM((tc + SUBLANES, 3 * H * GDN_DK), F32),
                        pltpu.VMEM((H, tc, GDN_DK), F32),
                        pltpu.VMEM((H, tc, GDN_DK), F32),
                        pltpu.VMEM((H, tc, GDN_DV), F32),
                        pltpu.VMEM((tc, LANES), F32),
                        pltpu.VMEM((tc, LANES), F32),
                        pltpu.VMEM((H, GDN_DK, GDN_DV), F32)],
        compiler_params=pltpu.CompilerParams(dimension_semantics=("arbitrary", "arbitrary"),
                                             vmem_limit_bytes=VMEM_LIMIT),
        name="gdn_mixer",
    )(qkv, z, small_a, conv_w, pad(a_log), pad(dt_bias), norm_w.reshape(1, GDN_DV))


def _compress_kernel(x_ref, pos_ref, w1_ref, w2_ref, o_ref, shift, *, nsub):
    half = CMP_STRIDE * NSA_DK
    x = x_ref[0, 0]
    pos = pos_ref[0]
    first = _dot((x + pos[:, :half]).astype(BF16), w1_ref[0, :half, :])
    second = _dot((x + pos[:, half:]).astype(BF16), w1_ref[0, half:, :])
    shift[0:nsub, :] = second
    shift[nsub:nsub + SUBLANES, :] = jnp.zeros((SUBLANES, CMP_HIDDEN), F32)
    hidden = first + shift[1:nsub + 1, :]
    y = _dot(_gelu_tanh(hidden).astype(BF16), w2_ref[0])
    row = lax.broadcasted_iota(jnp.int32, y.shape, 0)
    o_ref[0, 0] = jnp.where(row < nsub - 1, y, 0.0)


def _compress(cmp, pos_k, w1_k, w2_k, pos_v, w1_v, w2_v, batch, seq):
    G, d = NSA_GROUPS, NSA_DK
    nsub = seq // CMP_STRIDE
    xs = cmp.reshape(batch, nsub, CMP_STRIDE, 2 * G, d).transpose(0, 3, 1, 2, 4).reshape(batch, 2 * G, nsub, CMP_STRIDE * d)
    pos = jnp.stack([pos_k.reshape(1, -1), pos_v.reshape(1, -1)])
    w1 = jnp.stack([w1_k, w1_v]).astype(BF16)
    w2 = jnp.stack([w2_k, w2_v]).astype(BF16)
    out = pl.pallas_call(
        functools.partial(_compress_kernel, nsub=nsub),
        grid=(batch, 2 * G),
        in_specs=[pl.BlockSpec((1, 1, nsub, CMP_STRIDE * d), lambda b, c: (b, c, 0, 0)),
                  pl.BlockSpec((1, 1, CMP_LEN * d), lambda b, c: (c // G, 0, 0)),
                  pl.BlockSpec((1, CMP_LEN * d, CMP_HIDDEN), lambda b, c: (c // G, 0, 0)),
                  pl.BlockSpec((1, CMP_HIDDEN, d), lambda b, c: (c // G, 0, 0))],
        out_specs=pl.BlockSpec((1, 1, nsub, d), lambda b, c: (b, c, 0, 0)),
        out_shape=jax.ShapeDtypeStruct((batch, 2 * G, nsub, d), F32),
        scratch_shapes=[pltpu.VMEM((nsub + SUBLANES, CMP_HIDDEN), F32)],
        compiler_params=pltpu.CompilerParams(dimension_semantics=("arbitrary", "arbitrary"),
                                             vmem_limit_bytes=VMEM_LIMIT),
        name="nsa_compress",
    )(xs, pos, w1, w2)
    return jnp.concatenate([out[:, :G], out[:, G:]], axis=-1).astype(BF16)


NSA_MASK = -1e30
NSA_M0 = -1e29
NSA_VROWS = NSA_DV + 16
SEL, WIN = 0, 1
NSA_ONEHOT_ROW = 80
NSA_BIAS_ROWS = 16


def _nsa_kernel(q_ref, kcv_ref, ksel_ref, kwin_ref, gate_ref, slope_ref, o_ref,
                kc_aug, vc_t, k_all, v_all, q_sc, sbuf, negq, m_all, acc_all, *, tq, seq):
    R, DK, DV = NSA_HPG, NSA_DK, NSA_DV
    nsub = seq // CMP_STRIDE
    nsel = seq // SEL_BLOCK
    topn = min(SEL_TOPN, nsel)
    nql = R * tq
    per_step = tq // SEL_BLOCK
    nstep_sel = nsel // per_step
    pad = WINDOW // tq
    i = pl.program_id(2)
    t0 = i * tq
    d_kq = (lax.broadcasted_iota(jnp.int32, (tq, nql), 0)
            - (lax.broadcasted_iota(jnp.int32, (tq, nql), 1) & (tq - 1)))

    def aug_keys(kv, pos):
        lane = lax.broadcasted_iota(jnp.int32, kv.shape, 1)
        block_in_step = (pos >> _log2(SEL_BLOCK)) & (per_step - 1)
        onehot = jnp.where(lane - NSA_ONEHOT_ROW == block_in_step, 1.0, 0.0)
        aug = jnp.where(lane == DK, (pos >> 7).astype(F32),
                        jnp.where(lane == DK + 1, (pos & 127).astype(F32), jnp.where(lane == DK + 2, 1.0, onehot)))
        return jnp.where(lane < DK, kv, aug).astype(BF16)

    def values_t(kv):
        t = kv.T
        return jnp.concatenate([t[DK:DK + DV], jnp.ones((NSA_VROWS - DV, kv.shape[0]), F32)], axis=0).astype(BF16)

    @pl.when(i == 0)
    def _():
        lane = lax.broadcasted_iota(jnp.int32, (pad * tq, LANES), 1)
        pad_keys = jnp.where(lane == DK, NSA_MASK, 0.0).astype(BF16)
        for br, src in ((SEL, ksel_ref), (WIN, kwin_ref)):
            k_all[br, 0:pad * tq, :] = pad_keys
            v_all[br, 0:pad] = jnp.zeros((pad, NSA_VROWS, tq), BF16)

        def fill(c, carry):
            r0 = pl.multiple_of(c * tq, tq)
            pos = lax.broadcasted_iota(jnp.int32, (tq, LANES), 0) + r0
            for br, src in ((SEL, ksel_ref), (WIN, kwin_ref)):
                kv = src[pl.ds(r0, tq), :].astype(F32)
                k_all[br, pl.ds(r0 + pad * tq, tq), :] = aug_keys(kv, pos)
                v_all[br, c + pad] = values_t(kv)
            return carry

        lax.fori_loop(0, seq // tq, fill, 0)
        kcv = kcv_ref[0, 0].astype(F32)
        pos_c = lax.broadcasted_iota(jnp.int32, (nsub, LANES), 0) * CMP_STRIDE + (CMP_LEN - 1)
        kc_aug[...] = aug_keys(kcv, pos_c)
        vc_t[...] = kcv.T[DK:DK + DV].astype(BF16)
        negq[nstep_sel] = jnp.zeros((NSA_BIAS_ROWS, nql), BF16)

    row_q = lax.broadcasted_iota(jnp.int32, (LANES - DK, tq), 0)
    t0_f = (jnp.zeros((LANES - DK, tq), jnp.int32) + t0).astype(F32)
    q_all = q_ref[...].astype(F32).T
    qts = []
    for r in range(R):
        sl = slope_ref[0, r:r + 1, 0:1]
        aug = jnp.where(row_q == 0, sl * 128.0, jnp.where(row_q == 1, sl, jnp.where(row_q == 2, -sl * t0_f, 0.0)))
        qts.append(jnp.concatenate([q_all[r * DK:(r + 1) * DK], aug], axis=0))
    q_t = jnp.concatenate(qts, axis=1).astype(BF16)

    blk_end = lax.broadcasted_iota(jnp.int32, (nsub, nql), 0) * CMP_STRIDE + (CMP_LEN - 1)
    q_pos = (lax.broadcasted_iota(jnp.int32, (nsub, nql), 1) & (tq - 1)) + t0
    s = jnp.where(blk_end <= q_pos, _dot(kc_aug[...], q_t), NSA_MASK)
    m = jnp.maximum(jnp.max(s, axis=0, keepdims=True), NSA_M0)
    e = jnp.exp(s - m)
    den = jnp.sum(e, axis=0, keepdims=True)
    p = e * jnp.where(den > 0.0, 1.0 / den, 0.0)
    o_cmp = _dot(vc_t[...], p.astype(BF16))
    psum = p[:, 0:tq]
    for r in range(1, R):
        psum = psum + p[:, r * tq:(r + 1) * tq]

    nwin = WINDOW // tq + 1
    oh = NSA_ONEHOT_ROW

    def keys_of(br, step):
        return k_all[br, pl.ds(pl.multiple_of(step * tq, tq), tq), :]

    def accumulate(s, br, step):
        m_old = m_all[br]
        m_new = jnp.maximum(m_old, jnp.max(s, axis=0, keepdims=True))
        p = jnp.exp(s - m_new).astype(BF16)
        acc_all[br] = jnp.exp(m_old - m_new) * acc_all[br] + _dot(v_all[br, step], p)
        m_all[br] = m_new

    m_all[...] = jnp.full(m_all.shape, NSA_M0, F32)
    acc_all[...] = jnp.zeros(acc_all.shape, F32)

    s_win = [_dot(keys_of(WIN, i + w), q_t) for w in range(nwin)]
    s_win[0] = jnp.where(d_kq > 0, s_win[0], NSA_MASK)
    s_win[-1] = jnp.where(d_kq <= 0, s_win[-1], NSA_MASK)
    m_win = jnp.max(s_win[0], axis=0, keepdims=True)
    for s_w in s_win[1:]:
        m_win = jnp.maximum(m_win, jnp.max(s_w, axis=0, keepdims=True))
    pv_win = []

    def window_piece(w):
        pv_win.append(_dot(v_all[WIN, i + w], jnp.exp(s_win[w] - m_win).astype(BF16)))

    j_o = lax.broadcasted_iota(jnp.int32, (nsel, nsub), 0) * SEL_BLOCK
    n_o = lax.broadcasted_iota(jnp.int32, (nsel, nsub), 1) * CMP_STRIDE
    ov = jnp.maximum(jnp.minimum(n_o + CMP_LEN, j_o + SEL_BLOCK) - jnp.maximum(n_o, j_o), 0)
    ov = (ov.astype(F32) * (1.0 / CMP_LEN)).astype(BF16)
    p_hi = psum.astype(BF16)
    p_lo = (psum - p_hi.astype(F32)).astype(BF16)
    imp = _dot(ov, p_hi) + _dot(ov, p_lo)
    blk = lax.broadcasted_iota(jnp.int32, (nsel, tq), 0)
    cur = (lax.broadcasted_iota(jnp.int32, (nsel, tq), 1) + t0) >> _log2(SEL_BLOCK)
    valid_s = blk <= cur
    forced = (blk == 0) | (blk == cur) | (blk == cur - 1)
    score = jnp.where(valid_s, jnp.where(forced, FORCED_SCORE, imp), -jnp.inf)
    blk_f = blk.astype(F32)
    npieces = nwin
    done = 0
    for it in range(topn):
        m = jnp.max(score, axis=0, keepdims=True)
        first = jnp.min(jnp.where(score == m, blk_f, float(nsel)), axis=0, keepdims=True)
        score = jnp.where(blk_f == first, -jnp.inf, score)
        while done < ((it + 1) * npieces) // topn:
            window_piece(done)
            done += 1
    acc_all[WIN] = functools.reduce(lambda a, b: a + b, pv_win)
    m_all[WIN] = m_win
    nb = jnp.where(valid_s & (score == -jnp.inf), 0.0, NSA_MASK)
    for n in range(nstep_sel):
        rows = jnp.concatenate([nb[n * per_step:(n + 1) * per_step],
                                jnp.zeros((NSA_BIAS_ROWS - per_step, tq), F32)], axis=0)
        negq[n] = jnp.concatenate([rows] * R, axis=1).astype(BF16)

    q_diag = jnp.concatenate([q_t[:oh], negq[i], q_t[oh + NSA_BIAS_ROWS:]], axis=0)
    accumulate(jnp.where(d_kq <= 0, _dot(keys_of(SEL, i + pad), q_diag), NSA_MASK), SEL, i + pad)

    def schedule(n):
        dummy = n >= i
        return jnp.where(dummy, WIN, SEL), jnp.where(dummy, 0, n + pad), jnp.where(dummy, nstep_sel, n)

    def issue_scores(n, slot):
        br, step, group = schedule(n)
        q_sc[oh:oh + NSA_BIAS_ROWS, :] = negq[group]
        sbuf[slot] = _dot(keys_of(br, step), q_sc[...])

    def consume(n, slot):
        br, step, _ = schedule(n)
        accumulate(sbuf[slot], br, step)

    q_sc[...] = q_t
    issue_scores(0, 0)

    def body(m, carry):
        n = 2 * m
        issue_scores(n + 1, 1)
        consume(n, 0)
        issue_scores(n + 2, 0)
        consume(n + 1, 1)
        return carry

    lax.fori_loop(0, (i + 1) >> 1, body, 0)
    acc_s = acc_all[SEL]
    acc_w = acc_all[WIN]
    o_sel = acc_s[:DV] / acc_s[DV:DV + 1]
    o_win = acc_w[:DV] / acc_w[DV:DV + 1]

    g_t = _sigmoid(gate_ref[...]).T
    ys = []
    for r in range(R):
        cs = slice(r * tq, (r + 1) * tq)
        ys.append(g_t[3 * r:3 * r + 1] * o_cmp[:, cs] + g_t[3 * r + 1:3 * r + 2] * o_sel[:, cs]
                  + g_t[3 * r + 2:3 * r + 3] * o_win[:, cs])
    for half in range(R // 2):
        pair = jnp.concatenate([ys[2 * half], ys[2 * half + 1]], axis=0)
        o_ref[:, half * LANES:(half + 1) * LANES] = pair.T.astype(o_ref.dtype)


def _nsa_attention(qb, kcv, kv4, gate_b, batch, seq):
    t = qb.shape[0]
    G, R = NSA_GROUPS, NSA_HPG
    tq = TQ_NSA
    assert tq % LANES == 0 and WINDOW % tq == 0 and seq % tq == 0 and LANES == 2 * NSA_DK
    nq = seq // tq
    nsub = seq // CMP_STRIDE
    nsel = seq // SEL_BLOCK
    slopes = 2.0 ** (-8.0 * jnp.arange(1, NSA_HEADS + 1, dtype=F32) / NSA_HEADS)
    slope_rows = jnp.zeros((G, SUBLANES, LANES), F32).at[:, :R, :].set(
        jnp.broadcast_to(slopes.reshape(G, R, 1), (G, R, LANES)))
    return pl.pallas_call(
        functools.partial(_nsa_kernel, tq=tq, seq=seq),
        grid=(batch, G, nq),
        in_specs=[pl.BlockSpec((tq, R * NSA_DK), lambda b, g, i: (b * nq + i, g)),
                  pl.BlockSpec((1, 1, nsub, LANES), lambda b, g, i: (b, g, 0, 0)),
                  pl.BlockSpec((seq, LANES), lambda b, g, i: (b, g)),
                  pl.BlockSpec((seq, LANES), lambda b, g, i: (b, G + g)),
                  pl.BlockSpec((tq, LANES), lambda b, g, i: (b * nq + i, g)),
                  pl.BlockSpec((1, SUBLANES, LANES), lambda b, g, i: (g, 0, 0))],
        out_specs=pl.BlockSpec((tq, R * NSA_DV), lambda b, g, i: (b * nq + i, g)),
        out_shape=jax.ShapeDtypeStruct((t, NSA_HEADS * NSA_DV), BF16),
        scratch_shapes=[pltpu.VMEM((nsub, LANES), BF16),
                        pltpu.VMEM((NSA_DV, nsub), BF16),
                        pltpu.VMEM((2, seq + WINDOW, LANES), BF16),
                        pltpu.VMEM((2, (seq + WINDOW) // tq, NSA_VROWS, tq), BF16),
                        pltpu.VMEM((LANES, R * tq), BF16),
                        pltpu.VMEM((2, tq, R * tq), F32),
                        pltpu.VMEM((nsel // (tq // SEL_BLOCK) + 1, NSA_BIAS_ROWS, R * tq), BF16),
                        pltpu.VMEM((2, 1, R * tq), F32),
                        pltpu.VMEM((2, NSA_VROWS, R * tq), F32)],
        compiler_params=pltpu.CompilerParams(dimension_semantics=("arbitrary", "arbitrary", "arbitrary"),
                                             vmem_limit_bytes=VMEM_LIMIT),
        name="nsa_attention",
    )(qb, kcv, kv4, kv4, gate_b, slope_rows)


def _tail_kernel(x_ref, ya_ref, yb_ref, gm_ref, p_ref, wa_ref, wb_ref, wo_ref, nmix_ref, nfpre_ref,
                 wup_ref, cw_ref, cb_ref, wdn_ref, nfpost_ref, wple_ref, wpg_ref, nple_ref, o_ref,
                 ext, tail, act, *, tm, seq, d_ff):
    d = x_ref.shape[1]
    halo = FFN_CONV - 1

    @pl.when((pl.program_id(0) * tm) % seq == 0)
    def _():
        tail[...] = jnp.zeros(tail.shape, F32)

    nsub = ext.shape[0]
    ts = tm // nsub
    subs = [slice(s * ts, (s + 1) * ts) for s in range(nsub)]
    h1, u = [], []
    for rows in subs:
        mixed = (gm_ref[rows, :d].astype(F32) * _dot(ya_ref[rows, :], wa_ref[...])
                 + gm_ref[rows, d:].astype(F32) * _dot(yb_ref[rows, :], wb_ref[...]))
        h1.append(x_ref[rows, :] + _rms(_dot(mixed.astype(BF16), wo_ref[...]), nmix_ref[...]))
        u.append(_rms(h1[-1], nfpre_ref[...]).astype(BF16))

    for c in range(d_ff // FC_TAIL):
        fs = [[] for _ in range(nsub)]
        for part in range(2):
            cols = slice(part * d_ff + c * FC_TAIL, part * d_ff + (c + 1) * FC_TAIL)
            for s in range(nsub):
                ext[s, 0:SUBLANES, :] = tail[:, cols] if s == 0 else ext[s - 1, ts:ts + SUBLANES, :]
                ext[s, SUBLANES:SUBLANES + ts, :] = _dot(u[s], wup_ref[:, cols])
                f = cb_ref[:, cols] + cw_ref[halo:halo + 1, cols] * ext[s, SUBLANES:SUBLANES + ts, :]
                for k in range(halo):
                    r0 = SUBLANES - halo + k
                    f = f + cw_ref[k:k + 1, cols] * ext[s, r0:r0 + ts, :]
                fs[s].append(f)
            tail[:, cols] = ext[nsub - 1, ts:ts + SUBLANES, :]
        for s in range(nsub):
            act[s, :, c * FC_TAIL:(c + 1) * FC_TAIL] = (_gelu_tanh(fs[s][0]) * fs[s][1]).astype(BF16)

    for s, rows in enumerate(subs):
        h2 = h1[s] + _rms(_dot(act[s], wdn_ref[...]), nfpost_ref[...])
        e = _dot(p_ref[rows, :].astype(BF16), wple_ref[...]) * _sigmoid(_dot(h2.astype(BF16), wpg_ref[...]))
        o_ref[rows, :] = h2 + _rms(e, nple_ref[...])


def _tail(x2, ya, yb, gmix, p2, w_a2d, w_b2d, w_o, n_mix, n_fpre, w_up, conv_ffn, conv_b, w_down, n_fpost,
          w_ple, w_ple_gate, n_ple, seq):
    t, d = x2.shape
    d_ff = w_down.shape[0]
    tm = min(TM_TAIL, seq)
    tok = lambda w: pl.BlockSpec((tm, w), lambda i: (i, 0))
    row = lambda v: v.reshape(1, -1)
    b16 = lambda w: w.astype(BF16)
    consts = [b16(w_a2d), b16(w_b2d), b16(w_o), row(n_mix), row(n_fpre), b16(w_up), conv_ffn, row(conv_b),
              b16(w_down), row(n_fpost), b16(w_ple), b16(w_ple_gate), row(n_ple)]
    return pl.pallas_call(
        functools.partial(_tail_kernel, tm=tm, seq=seq, d_ff=d_ff),
        grid=(t // tm,),
        in_specs=[tok(d), tok(ya.shape[1]), tok(yb.shape[1]), tok(2 * d), tok(p2.shape[1])]
                 + [_resident(c.shape) for c in consts],
        out_specs=tok(d),
        out_shape=jax.ShapeDtypeStruct((t, d), F32),
        scratch_shapes=[pltpu.VMEM((NSUB_TAIL, tm // NSUB_TAIL + SUBLANES, FC_TAIL), F32),
                        pltpu.VMEM((SUBLANES, 2 * d_ff), F32),
                        pltpu.VMEM((NSUB_TAIL, tm // NSUB_TAIL, d_ff), BF16)],
        compiler_params=pltpu.CompilerParams(dimension_semantics=("arbitrary",), vmem_limit_bytes=VMEM_LIMIT),
        name="tail_ffn",
    )(x2, ya, yb, gmix, p2, *consts)


def _layer(h2, p2, batch, seq, norm_mix_pre, w_in, conv_qkv, a_log, dt_bias, gdn_norm,
           cmp_pos_k, cmp_w1_k, cmp_w2_k, cmp_pos_v, cmp_w1_v, cmp_w2_v,
           w_a2d, w_b2d, w_o, norm_mix_post, norm_ffn_pre, w_up, conv_ffn, conv_ffn_b,
           w_down, norm_ffn_post, w_ple, w_ple_gate, norm_ple_post):
    qkv, z, qb, cmp, kv4, gmix, small_a, gate_b = _in_proj(h2, norm_mix_pre, _prep_w_in(w_in))
    ya = _gdn(qkv, z, small_a, conv_qkv, a_log, dt_bias, gdn_norm, batch, seq)
    kcv = _compress(cmp, cmp_pos_k, cmp_w1_k, cmp_w2_k, cmp_pos_v, cmp_w1_v, cmp_w2_v, batch, seq)
    yb = _nsa_attention(qb, kcv, kv4, gate_b, batch, seq)
    return _tail(h2, ya, yb, gmix, p2, w_a2d, w_b2d, w_o, norm_mix_post, norm_ffn_pre, w_up, conv_ffn,
                 conv_ffn_b, w_down, norm_ffn_post, w_ple, w_ple_gate, norm_ple_post, seq)


def kernel(x, p, norm_mix_pre, w_in, conv_qkv, a_log, dt_bias, gdn_norm, cmp_pos_k, cmp_w1_k, cmp_w2_k, cmp_pos_v, cmp_w1_v, cmp_w2_v, w_a2d, w_b2d, w_o, norm_mix_post, norm_ffn_pre, w_up, conv_ffn, conv_ffn_b, w_down, norm_ffn_post, w_ple, w_ple_gate, norm_ple_post):
    batch, seq, d = x.shape
    params = (norm_mix_pre, w_in, conv_qkv, a_log, dt_bias, gdn_norm, cmp_pos_k, cmp_w1_k, cmp_w2_k,
              cmp_pos_v, cmp_w1_v, cmp_w2_v, w_a2d, w_b2d, w_o, norm_mix_post, norm_ffn_pre, w_up,
              conv_ffn, conv_ffn_b, w_down, norm_ffn_post, w_ple, w_ple_gate, norm_ple_post)
    h = x.reshape(batch * seq, d)
    for i in range(p.shape[0]):
        h = _layer(h, p[i].reshape(batch * seq, -1), batch, seq, *[w[i] for w in params])
    return h.reshape(batch, seq, d)
```

```python
import functools

import numpy as np
import jax
import jax.numpy as jnp
from jax import lax
from jax.experimental import pallas as pl
from jax.experimental.pallas import tpu as pltpu

F32 = jnp.float32
BF16 = jnp.bfloat16
EPS = 1e-6
NEG = -1e30
HIGHEST = lax.Precision.HIGHEST

GDN_HEADS, GDN_DK, GDN_DV, GDN_CONV, GDN_CHUNK = 4, 128, 128, 4, 64
NSA_HEADS, NSA_GROUPS, NSA_DK, NSA_DV = 8, 2, 64, 64
NSA_HPG = NSA_HEADS // NSA_GROUPS
CMP_LEN, CMP_STRIDE, CMP_HIDDEN = 32, 16, 256
SEL_BLOCK, SEL_TOPN = 64, 16
WINDOW = 512
FORCED_SCORE = 1e9
FFN_CONV = 3
IN_SPLITS = (512, 512, 512, 512, 4, 4, 512, 128, 128, 128, 128, 128, 128, 24, 2048)

LANES = 128
SUBLANES = 8
VMEM_LIMIT = 56 * 1024 * 1024

TM_PROJ = 512
TC_GDN = 256
TQ_NSA = 256
TM_TAIL = 512
NSUB_TAIL = 2
FC_TAIL = 256


def _log2(n):
    assert n & (n - 1) == 0
    return n.bit_length() - 1


def _dot(a, b, precision=None):
    return jnp.dot(a, b, preferred_element_type=F32, precision=precision)


def _dot_nt(a, b):
    return lax.dot_general(a, b, (((1,), (1,)), ((), ())), preferred_element_type=F32)


def _rms(x, w):
    return x * lax.rsqrt(jnp.mean(x * x, axis=-1, keepdims=True) + EPS) * w


def _sigmoid(x):
    return 1.0 / (1.0 + jnp.exp(-x))


def _silu(x):
    return x * _sigmoid(x)


def _gelu_tanh(x):
    c = 0.7978845608028654
    half = 0.5 * x
    return half + half * jnp.tanh(x * (c + (c * 0.044715) * (x * x)))


def _softplus(x):
    return jnp.maximum(x, 0.0) + jnp.log1p(jnp.exp(-jnp.abs(x)))


def _resident(shape):
    zeros = (0,) * len(shape)
    return pl.BlockSpec(shape, lambda *_: zeros, pipeline_mode=pl.Buffered(1))


PROJ_WIDTHS = (1536, 512, 512, 256, 512, 2048, 128)
PROJ_DTYPES = (F32, F32, BF16, BF16, BF16, BF16, F32)
PROJ_GATE = (False, False, False, False, False, True, False)
PROJ_COLS = 512


def _in_proj_kernel(x_ref, nw_ref, w_ref, *out_refs):
    x = x_ref[...]
    u = _rms(x, nw_ref[...]).astype(BF16)
    off = 0
    for o_ref, width, gate in zip(out_refs, PROJ_WIDTHS, PROJ_GATE):
        for c0 in range(0, width, PROJ_COLS):
            c1 = min(c0 + PROJ_COLS, width)
            y = _dot(u, w_ref[:, off + c0:off + c1])
            o_ref[:, c0:c1] = (_sigmoid(y) if gate else y).astype(o_ref.dtype)
        off += width


def _prep_w_in(w_in):
    offs = np.cumsum((0,) + IN_SPLITS)
    w16 = w_in.astype(BF16)
    qa, ka, va, za, ba, aa, qb, kcm, vcm, ksl, vsl, kwi, vwi, gnsa, gmix = [
        w16[:, offs[i]:offs[i + 1]] for i in range(len(IN_SPLITS))]
    qb = qb * (NSA_DK ** -0.5)
    grp = lambda y, g: y[:, g * NSA_DK:(g + 1) * NSA_DK]
    kv4 = jnp.concatenate([grp(ksl, 0), grp(vsl, 0), grp(ksl, 1), grp(vsl, 1),
                           grp(kwi, 0), grp(vwi, 0), grp(kwi, 1), grp(vwi, 1)], axis=1)
    used = 2 * GDN_HEADS + NSA_HEADS * 3
    small = jnp.concatenate([ba, aa, gnsa, jnp.zeros((w_in.shape[0], LANES - used), BF16)], axis=1)
    w = jnp.concatenate([qa, ka, va, za, qb, kcm, vcm, kv4, gmix, small], axis=1)
    assert w.shape[1] == sum(PROJ_WIDTHS)
    return w


def _in_proj(x2, norm_w, w_prep):
    t, d = x2.shape
    tm = min(TM_PROJ, t)
    n = w_prep.shape[1]
    return pl.pallas_call(
        _in_proj_kernel,
        grid=(t // tm,),
        in_specs=[pl.BlockSpec((tm, d), lambda i: (i, 0)),
                  _resident((1, d)),
                  _resident((d, n))],
        out_specs=[pl.BlockSpec((tm, wd), lambda i: (i, 0)) for wd in PROJ_WIDTHS],
        out_shape=[jax.ShapeDtypeStruct((t, wd), dt) for wd, dt in zip(PROJ_WIDTHS, PROJ_DTYPES)],
        compiler_params=pltpu.CompilerParams(dimension_semantics=("arbitrary",), vmem_limit_bytes=VMEM_LIMIT),
        name="in_proj",
    )(x2, norm_w.reshape(1, d), w_prep)


def _gdn_kernel(qkv_ref, z_ref, sm_ref, cw_ref, alog_ref, dtb_ref, nw_ref, o_ref,
                xbuf, qn, kn, vn, gcs, bts, state, *, tc):
    H, DK, C = GDN_HEADS, GDN_DK, GDN_CHUNK

    @pl.when(pl.program_id(1) == 0)
    def _():
        xbuf[0:SUBLANES, :] = jnp.zeros((SUBLANES, xbuf.shape[1]), F32)
        state[...] = jnp.zeros(state.shape, F32)

    xbuf[SUBLANES:SUBLANES + tc, :] = qkv_ref[...]

    for blk in range(3 * H):
        cs = slice(blk * DK, (blk + 1) * DK)
        conv = cw_ref[GDN_CONV - 1:GDN_CONV, cs] * xbuf[SUBLANES:SUBLANES + tc, cs]
        for kk in range(GDN_CONV - 1):
            r0 = SUBLANES - (GDN_CONV - 1) + kk
            conv = conv + cw_ref[kk:kk + 1, cs] * xbuf[r0:r0 + tc, cs]
        act = _silu(conv)
        which, h = divmod(blk, H)
        if which == 0:
            qn[h] = act * lax.rsqrt(jnp.sum(act * act, axis=-1, keepdims=True) + EPS) * (DK ** -0.5)
        elif which == 1:
            kn[h] = act * lax.rsqrt(jnp.sum(act * act, axis=-1, keepdims=True) + EPS)
        else:
            vn[h] = act
    xbuf[0:SUBLANES, :] = xbuf[tc:tc + SUBLANES, :]

    sm = sm_ref[...]
    bts[...] = _sigmoid(sm)
    g_all = -jnp.exp(alog_ref[...]) * _softplus(sm + dtb_ref[...])
    ri = lax.broadcasted_iota(jnp.int32, (tc, tc), 0)
    ci = lax.broadcasted_iota(jnp.int32, (tc, tc), 1)
    same_chunk = (ri >> _log2(C)) == (ci >> _log2(C))
    block_tril = jnp.where(same_chunk & (ri >= ci), 1.0, 0.0).astype(F32)
    gcs[...] = _dot(block_tril, g_all, precision=HIGHEST)

    i64 = lax.broadcasted_iota(jnp.int32, (C, C), 0)
    j64 = lax.broadcasted_iota(jnp.int32, (C, C), 1)
    incl = i64 >= j64
    strict = i64 > j64
    eye = jnp.where(i64 == j64, 1.0, 0.0).astype(F32)
    pair_masks = [((i64 >> (lv + 1)) == (j64 >> (lv + 1))) & ((i64 >> lv) != (j64 >> lv)) for lv in range(_log2(C))]
    nw = nw_ref[...]

    units = [(c, h) for c in range(tc // C) for h in range(H)]
    rows = [slice(c * C, (c + 1) * C) for c in range(tc // C)]
    gc = [gcs[r, :] for r in rows]
    bt = [bts[r, :] for r in rows]
    gc_t = [x.T for x in gc]
    bt_t = [x.T for x in bt]
    gc_col = [gc[c][:, H + h:H + h + 1] for c, h in units]
    g_last = [gc[c][C - 1:C, H + h:H + h + 1] for c, h in units]
    e = [jnp.exp(jnp.where(incl, gc_col[n] - gc_t[c][H + h:H + h + 1, :], 0.0)) for n, (c, h) in enumerate(units)]
    q = [qn[h, rows[c], :] for c, h in units]
    k = [kn[h, rows[c], :] for c, h in units]
    kb = [x.astype(BF16) for x in k]
    kkt = [_dot_nt(x, x) for x in kb]
    qkm = [(_dot_nt(q[n].astype(BF16), kb[n]) * jnp.where(incl, e[n], 0.0)).astype(BF16) for n in range(len(units))]
    low = [bt[c][:, h:h + 1] * kkt[n] * jnp.where(strict, e[n], 0.0) for n, (c, h) in enumerate(units)]
    acc = [-jnp.where(pair_masks[0], x, 0.0) for x in low]
    for pm in pair_masks[1:]:
        off = [jnp.where(pm, x, 0.0) for x in low]
        tb = [off[n] + _dot(acc[n].astype(BF16), off[n].astype(BF16)) for n in range(len(units))]
        acc = [acc[n] - tb[n] - _dot(tb[n].astype(BF16), acc[n].astype(BF16)) for n in range(len(units))]
    t_beta = [((eye + acc[n]) * bt_t[c][h:h + 1, :]).astype(BF16) for n, (c, h) in enumerate(units)]
    gam = [jnp.exp(x) for x in gc_col]
    sol = [_dot(t_beta[n], jnp.concatenate([k[n] * gam[n], vn[h, rows[c], :]], axis=1).astype(BF16))
           for n, (c, h) in enumerate(units)]
    wq = [jnp.concatenate([sol[n][:, :DK], q[n] * gam[n]], axis=0).astype(BF16) for n in range(len(units))]
    kt_t = [(k[n] * jnp.exp(g_last[n] - gc_col[n])).T.astype(BF16) for n in range(len(units))]
    s_decay = [jnp.exp(x) for x in g_last]

    heads = range(H)
    s_cur = [state[h] for h in heads]
    for c in range(tc // C):
        ns = [c * H + h for h in heads]
        ws = [_dot(wq[n], s_cur[h].astype(BF16)) for h, n in zip(heads, ns)]
        ub = [(sol[n][:, DK:] - ws[h][:C]).astype(BF16) for h, n in zip(heads, ns)]
        o = [ws[h][C:] + _dot(qkm[n], ub[h]) for h, n in zip(heads, ns)]
        s_cur = [s_decay[n] * s_cur[h] + _dot(kt_t[n], ub[h]) for h, n in zip(heads, ns)]
        for h in heads:
            y = _rms(o[h], nw) * _silu(z_ref[rows[c], h * GDN_DV:(h + 1) * GDN_DV])
            o_ref[rows[c], h * GDN_DV:(h + 1) * GDN_DV] = y.astype(o_ref.dtype)
    for h in heads:
        state[h] = s_cur[h]


def _gdn(qkv, z, small_a, conv_w, a_log, dt_bias, norm_w, batch, seq):
    t = qkv.shape[0]
    tc = min(TC_GDN, seq)
    nj = seq // tc
    H = GDN_HEADS
    pad = lambda v: jnp.zeros((1, LANES), F32).at[0, H:2 * H].set(v.astype(F32))
    tok = lambda w: pl.BlockSpec((tc, w), lambda b, j: (b * nj + j, 0))
    return pl.pallas_call(
        functools.partial(_gdn_kernel, tc=tc),
        grid=(batch, nj),
        in_specs=[tok(3 * H * GDN_DK), tok(H * GDN_DV), tok(LANES),
                  _resident(conv_w.shape), _resident((1, LANES)), _resident((1, LANES)),
                  _resident((1, GDN_DV))],
        out_specs=tok(H * GDN_DV),
        out_shape=jax.ShapeDtypeStruct((t, H * GDN_DV), BF16),
        scratch_shapes=[pltpu.VMEM((tc + SUBLANES, 3 * H * GDN_DK), F32),
                        pltpu.VMEM((H, tc, GDN_DK), F32),
                        pltpu.VMEM((H, tc, GDN_DK), F32),
                        pltpu.VMEM((H, tc, GDN_DV), F32),
                        pltpu.VMEM((tc, LANES), F32),
                        pltpu.VMEM((tc, LANES), F32),
                        pltpu.VMEM((H, GDN_DK, GDN_DV), F32)],
        compiler_params=pltpu.CompilerParams(dimension_semantics=("arbitrary", "arbitrary"),
                                             vmem_limit_bytes=VMEM_LIMIT),
        name="gdn_mixer",
    )(qkv, z, small_a, conv_w, pad(a_log), pad(dt_bias), norm_w.reshape(1, GDN_DV))


def _compress_kernel(x_ref, pos_ref, w1_ref, w2_ref, o_ref, shift, *, nsub):
    half = CMP_STRIDE * NSA_DK
    x = x_ref[0, 0]
    pos = jnp.broadcast_to(pos_ref[0], (SUBLANES, 2 * half)).astype(BF16)
    first = _dot(x, w1_ref[0, :half, :]) + _dot(pos[:, :half], w1_ref[0, :half, :])[0:1]
    second = _dot(x, w1_ref[0, half:, :]) + _dot(pos[:, half:], w1_ref[0, half:, :])[0:1]
    shift[0:nsub, :] = second
    shift[nsub:nsub + SUBLANES, :] = jnp.zeros((SUBLANES, CMP_HIDDEN), F32)
    hidden = first + shift[1:nsub + 1, :]
    y = _dot(_gelu_tanh(hidden).astype(BF16), w2_ref[0])
    row = lax.broadcasted_iota(jnp.int32, y.shape, 0)
    o_ref[0, 0] = jnp.where(row < nsub - 1, y, 0.0)


def _compress(cmp, pos_k, w1_k, w2_k, pos_v, w1_v, w2_v, batch, seq):
    G, d = NSA_GROUPS, NSA_DK
    nsub = seq // CMP_STRIDE
    xs = cmp.reshape(batch, nsub, CMP_STRIDE, 2 * G, d).transpose(0, 3, 1, 2, 4).reshape(batch, 2 * G, nsub, CMP_STRIDE * d)
    pos = jnp.stack([pos_k.reshape(1, -1), pos_v.reshape(1, -1)])
    w1 = jnp.stack([w1_k, w1_v]).astype(BF16)
    w2 = jnp.stack([w2_k, w2_v]).astype(BF16)
    out = pl.pallas_call(
        functools.partial(_compress_kernel, nsub=nsub),
        grid=(batch, 2 * G),
        in_specs=[pl.BlockSpec((1, 1, nsub, CMP_STRIDE * d), lambda b, c: (b, c, 0, 0)),
                  pl.BlockSpec((1, 1, CMP_LEN * d), lambda b, c: (c // G, 0, 0)),
                  pl.BlockSpec((1, CMP_LEN * d, CMP_HIDDEN), lambda b, c: (c // G, 0, 0)),
                  pl.BlockSpec((1, CMP_HIDDEN, d), lambda b, c: (c // G, 0, 0))],
        out_specs=pl.BlockSpec((1, 1, nsub, d), lambda b, c: (b, c, 0, 0)),
        out_shape=jax.ShapeDtypeStruct((batch, 2 * G, nsub, d), F32),
        scratch_shapes=[pltpu.VMEM((nsub + SUBLANES, CMP_HIDDEN), F32)],
        compiler_params=pltpu.CompilerParams(dimension_semantics=("arbitrary", "arbitrary"),
                                             vmem_limit_bytes=VMEM_LIMIT),
        name="nsa_compress",
    )(xs, pos, w1, w2)
    return jnp.concatenate([out[:, :G], out[:, G:]], axis=-1).astype(BF16)


NSA_MASK = -1e30
NSA_M0 = -1e29
NSA_VROWS = NSA_DV + 16
SEL, WIN = 0, 1
NSA_ONEHOT_ROW = 80
NSA_BIAS_ROWS = 16


def _nsa_kernel(q_ref, kcv_ref, ksel_ref, kwin_ref, gate_ref, slope_ref, o_ref,
                kc_aug, vc_t, k_all, v_all, q_sc, sbuf, negq, m_all, acc_all, *, tq, seq):
    R, DK, DV = NSA_HPG, NSA_DK, NSA_DV
    nsub = seq // CMP_STRIDE
    nsel = seq // SEL_BLOCK
    topn = min(SEL_TOPN, nsel)
    nql = R * tq
    per_step = tq // SEL_BLOCK
    nstep_sel = nsel // per_step
    pad = WINDOW // tq
    i = pl.program_id(2)
    t0 = i * tq
    d_kq = (lax.broadcasted_iota(jnp.int32, (tq, nql), 0)
            - (lax.broadcasted_iota(jnp.int32, (tq, nql), 1) & (tq - 1)))

    def aug_keys(kv, pos):
        lane = lax.broadcasted_iota(jnp.int32, kv.shape, 1)
        block_in_step = (pos >> _log2(SEL_BLOCK)) & (per_step - 1)
        onehot = jnp.where(lane - NSA_ONEHOT_ROW == block_in_step, 1.0, 0.0)
        aug = jnp.where(lane == DK, (pos >> 7).astype(F32),
                        jnp.where(lane == DK + 1, (pos & 127).astype(F32), jnp.where(lane == DK + 2, 1.0, onehot)))
        return jnp.where(lane < DK, kv, aug).astype(BF16)

    def values_t(kv):
        t = kv.T
        return jnp.concatenate([t[DK:DK + DV], jnp.ones((NSA_VROWS - DV, kv.shape[0]), F32)], axis=0).astype(BF16)

    @pl.when(i == 0)
    def _():
        lane = lax.broadcasted_iota(jnp.int32, (pad * tq, LANES), 1)
        pad_keys = jnp.where(lane == DK, NSA_MASK, 0.0).astype(BF16)
        for br, src in ((SEL, ksel_ref), (WIN, kwin_ref)):
            k_all[br, 0:pad * tq, :] = pad_keys
            v_all[br, 0:pad] = jnp.zeros((pad, NSA_VROWS, tq), BF16)

        def fill(c, carry):
            r0 = pl.multiple_of(c * tq, tq)
            pos = lax.broadcasted_iota(jnp.int32, (tq, LANES), 0) + r0
            for br, src in ((SEL, ksel_ref), (WIN, kwin_ref)):
                kv = src[pl.ds(r0, tq), :].astype(F32)
                k_all[br, pl.ds(r0 + pad * tq, tq), :] = aug_keys(kv, pos)
                v_all[br, c + pad] = values_t(kv)
            return carry

        lax.fori_loop(0, seq // tq, fill, 0)
        kcv = kcv_ref[0, 0].astype(F32)
        pos_c = lax.broadcasted_iota(jnp.int32, (nsub, LANES), 0) * CMP_STRIDE + (CMP_LEN - 1)
        kc_aug[...] = aug_keys(kcv, pos_c)
        vc_t[...] = kcv.T[DK:DK + DV].astype(BF16)
        negq[nstep_sel] = jnp.zeros((NSA_BIAS_ROWS, nql), BF16)

    row_q = lax.broadcasted_iota(jnp.int32, (LANES - DK, tq), 0)
    t0_f = (jnp.zeros((LANES - DK, tq), jnp.int32) + t0).astype(F32)
    q_all = q_ref[...].astype(F32).T
    qts = []
    for r in range(R):
        sl = slope_ref[0, r:r + 1, 0:1]
        aug = jnp.where(row_q == 0, sl * 128.0, jnp.where(row_q == 1, sl, jnp.where(row_q == 2, -sl * t0_f, 0.0)))
        qts.append(jnp.concatenate([q_all[r * DK:(r + 1) * DK], aug], axis=0))
    q_t = jnp.concatenate(qts, axis=1).astype(BF16)

    blk_end = lax.broadcasted_iota(jnp.int32, (nsub, nql), 0) * CMP_STRIDE + (CMP_LEN - 1)
    q_pos = (lax.broadcasted_iota(jnp.int32, (nsub, nql), 1) & (tq - 1)) + t0
    s = jnp.where(blk_end <= q_pos, _dot(kc_aug[...], q_t), NSA_MASK)
    m = jnp.maximum(jnp.max(s, axis=0, keepdims=True), NSA_M0)
    e = jnp.exp(s - m)
    den = jnp.sum(e, axis=0, keepdims=True)
    p = e * jnp.where(den > 0.0, 1.0 / den, 0.0)
    o_cmp = _dot(vc_t[...], p.astype(BF16))
    psum = p[:, 0:tq]
    for r in range(1, R):
        psum = psum + p[:, r * tq:(r + 1) * tq]

    nwin = WINDOW // tq + 1
    oh = NSA_ONEHOT_ROW

    def keys_of(br, step):
        return k_all[br, pl.ds(pl.multiple_of(step * tq, tq), tq), :]

    def accumulate(s, br, step):
        m_old = m_all[br]
        m_new = jnp.maximum(m_old, jnp.max(s, axis=0, keepdims=True))
        p = jnp.exp(s - m_new).astype(BF16)
        acc_all[br] = jnp.exp(m_old - m_new) * acc_all[br] + _dot(v_all[br, step], p)
        m_all[br] = m_new

    m_all[...] = jnp.full(m_all.shape, NSA_M0, F32)
    acc_all[...] = jnp.zeros(acc_all.shape, F32)

    s_win = [_dot(keys_of(WIN, i + w), q_t) for w in range(nwin)]
    s_win[0] = jnp.where(d_kq > 0, s_win[0], NSA_MASK)
    s_win[-1] = jnp.where(d_kq <= 0, s_win[-1], NSA_MASK)
    m_win = jnp.max(s_win[0], axis=0, keepdims=True)
    for s_w in s_win[1:]:
        m_win = jnp.maximum(m_win, jnp.max(s_w, axis=0, keepdims=True))
    pv_win = []

    def window_piece(w):
        pv_win.append(_dot(v_all[WIN, i + w], jnp.exp(s_win[w] - m_win).astype(BF16)))

    j_o = lax.broadcasted_iota(jnp.int32, (nsel, nsub), 0) * SEL_BLOCK
    n_o = lax.broadcasted_iota(jnp.int32, (nsel, nsub), 1) * CMP_STRIDE
    ov = jnp.maximum(jnp.minimum(n_o + CMP_LEN, j_o + SEL_BLOCK) - jnp.maximum(n_o, j_o), 0)
    ov = (ov.astype(F32) * (1.0 / CMP_LEN)).astype(BF16)
    p_hi = psum.astype(BF16)
    p_lo = (psum - p_hi.astype(F32)).astype(BF16)
    imp = _dot(ov, p_hi) + _dot(ov, p_lo)
    blk = lax.broadcasted_iota(jnp.int32, (nsel, tq), 0)
    cur = (lax.broadcasted_iota(jnp.int32, (nsel, tq), 1) + t0) >> _log2(SEL_BLOCK)
    valid_s = blk <= cur
    forced = (blk == 0) | (blk == cur) | (blk == cur - 1)
    score = jnp.where(valid_s, jnp.where(forced, FORCED_SCORE, imp), -jnp.inf)
    blk_f = blk.astype(F32)
    npieces = nwin
    done = 0
    for it in range(topn):
        m = jnp.max(score, axis=0, keepdims=True)
        first = jnp.min(jnp.where(score == m, blk_f, float(nsel)), axis=0, keepdims=True)
        score = jnp.where(blk_f == first, -jnp.inf, score)
        while done < ((it + 1) * npieces) // topn:
            window_piece(done)
            done += 1
    acc_all[WIN] = functools.reduce(lambda a, b: a + b, pv_win)
    m_all[WIN] = m_win
    nb = jnp.where(valid_s & (score == -jnp.inf), 0.0, NSA_MASK)
    for n in range(nstep_sel):
        rows = jnp.concatenate([nb[n * per_step:(n + 1) * per_step],
                                jnp.zeros((NSA_BIAS_ROWS - per_step, tq), F32)], axis=0)
        negq[n] = jnp.concatenate([rows] * R, axis=1).astype(BF16)

    q_diag = jnp.concatenate([q_t[:oh], negq[i], q_t[oh + NSA_BIAS_ROWS:]], axis=0)
    accumulate(jnp.where(d_kq <= 0, _dot(keys_of(SEL, i + pad), q_diag), NSA_MASK), SEL, i + pad)

    def schedule(n):
        dummy = n >= i
        return jnp.where(dummy, WIN, SEL), jnp.where(dummy, 0, n + pad), jnp.where(dummy, nstep_sel, n)

    def issue_scores(n, slot):
        br, step, group = schedule(n)
        q_sc[oh:oh + NSA_BIAS_ROWS, :] = negq[group]
        sbuf[slot] = _dot(keys_of(br, step), q_sc[...])

    def consume(n, slot):
        br, step, _ = schedule(n)
        accumulate(sbuf[slot], br, step)

    q_sc[...] = q_t
    issue_scores(0, 0)

    def body(m, carry):
        n = 2 * m
        issue_scores(n + 1, 1)
        consume(n, 0)
        issue_scores(n + 2, 0)
        consume(n + 1, 1)
        return carry

    lax.fori_loop(0, (i + 1) >> 1, body, 0)
    acc_s = acc_all[SEL]
    acc_w = acc_all[WIN]
    o_sel = acc_s[:DV] / acc_s[DV:DV + 1]
    o_win = acc_w[:DV] / acc_w[DV:DV + 1]

    g_all = _sigmoid(gate_ref[...]).T
    g0 = 2 * GDN_HEADS
    g_t = jnp.where(pl.program_id(1) == 0, g_all[g0:g0 + 3 * R], g_all[g0 + 3 * R:g0 + 6 * R])
    ys = []
    for r in range(R):
        cs = slice(r * tq, (r + 1) * tq)
        ys.append(g_t[3 * r:3 * r + 1] * o_cmp[:, cs] + g_t[3 * r + 1:3 * r + 2] * o_sel[:, cs]
                  + g_t[3 * r + 2:3 * r + 3] * o_win[:, cs])
    for half in range(R // 2):
        pair = jnp.concatenate([ys[2 * half], ys[2 * half + 1]], axis=0)
        o_ref[:, half * LANES:(half + 1) * LANES] = pair.T.astype(o_ref.dtype)


def _nsa_attention(qb, kcv, kv4, small, batch, seq):
    t = qb.shape[0]
    G, R = NSA_GROUPS, NSA_HPG
    tq = TQ_NSA
    assert tq % LANES == 0 and WINDOW % tq == 0 and seq % tq == 0 and LANES == 2 * NSA_DK
    nq = seq // tq
    nsub = seq // CMP_STRIDE
    nsel = seq // SEL_BLOCK
    slopes = 2.0 ** (-8.0 * jnp.arange(1, NSA_HEADS + 1, dtype=F32) / NSA_HEADS)
    slope_rows = jnp.zeros((G, SUBLANES, LANES), F32).at[:, :R, :].set(
        jnp.broadcast_to(slopes.reshape(G, R, 1), (G, R, LANES)))
    return pl.pallas_call(
        functools.partial(_nsa_kernel, tq=tq, seq=seq),
        grid=(batch, G, nq),
        in_specs=[pl.BlockSpec((tq, R * NSA_DK), lambda b, g, i: (b * nq + i, g)),
                  pl.BlockSpec((1, 1, nsub, LANES), lambda b, g, i: (b, g, 0, 0)),
                  pl.BlockSpec((seq, LANES), lambda b, g, i: (b, g)),
                  pl.BlockSpec((seq, LANES), lambda b, g, i: (b, G + g)),
                  pl.BlockSpec((tq, LANES), lambda b, g, i: (b * nq + i, 0)),
                  pl.BlockSpec((1, SUBLANES, LANES), lambda b, g, i: (g, 0, 0))],
        out_specs=pl.BlockSpec((tq, R * NSA_DV), lambda b, g, i: (b * nq + i, g)),
        out_shape=jax.ShapeDtypeStruct((t, NSA_HEADS * NSA_DV), BF16),
        scratch_shapes=[pltpu.VMEM((nsub, LANES), BF16),
                        pltpu.VMEM((NSA_DV, nsub), BF16),
                        pltpu.VMEM((2, seq + WINDOW, LANES), BF16),
                        pltpu.VMEM((2, (seq + WINDOW) // tq, NSA_VROWS, tq), BF16),
                        pltpu.VMEM((LANES, R * tq), BF16),
                        pltpu.VMEM((2, tq, R * tq), F32),
                        pltpu.VMEM((nsel // (tq // SEL_BLOCK) + 1, NSA_BIAS_ROWS, R * tq), BF16),
                        pltpu.VMEM((2, 1, R * tq), F32),
                        pltpu.VMEM((2, NSA_VROWS, R * tq), F32)],
        compiler_params=pltpu.CompilerParams(dimension_semantics=("arbitrary", "arbitrary", "arbitrary"),
                                             vmem_limit_bytes=VMEM_LIMIT),
        name="nsa_attention",
    )(qb, kcv, kv4, kv4, small, slope_rows)


def _tail_kernel(x_ref, ya_ref, yb_ref, gm_ref, p_ref, wa_ref, wb_ref, wo_ref, nmix_ref, nfpre_ref,
                 wup_ref, cw_ref, cb_ref, wdn_ref, nfpost_ref, wple_ref, wpg_ref, nple_ref, o_ref,
                 ext, tail, act, *, tm, seq, d_ff):
    d = x_ref.shape[1]
    halo = FFN_CONV - 1

    @pl.when((pl.program_id(0) * tm) % seq == 0)
    def _():
        tail[...] = jnp.zeros(tail.shape, F32)

    nsub = ext.shape[0]
    ts = tm // nsub
    subs = [slice(s * ts, (s + 1) * ts) for s in range(nsub)]
    h1, u = [], []
    for rows in subs:
        mixed = (gm_ref[rows, :d].astype(F32) * _dot(ya_ref[rows, :], wa_ref[...])
                 + gm_ref[rows, d:].astype(F32) * _dot(yb_ref[rows, :], wb_ref[...]))
        h1.append(x_ref[rows, :] + _rms(_dot(mixed.astype(BF16), wo_ref[...]), nmix_ref[...]))
        u.append(_rms(h1[-1], nfpre_ref[...]).astype(BF16))

    for c in range(d_ff // FC_TAIL):
        fs = [[] for _ in range(nsub)]
        for part in range(2):
            cols = slice(part * d_ff + c * FC_TAIL, part * d_ff + (c + 1) * FC_TAIL)
            for s in range(nsub):
                ext[s, 0:SUBLANES, :] = tail[:, cols] if s == 0 else ext[s - 1, ts:ts + SUBLANES, :]
                ext[s, SUBLANES:SUBLANES + ts, :] = _dot(u[s], wup_ref[:, cols])
                f = cb_ref[:, cols] + cw_ref[halo:halo + 1, cols] * ext[s, SUBLANES:SUBLANES + ts, :]
                for k in range(halo):
                    r0 = SUBLANES - halo + k
                    f = f + cw_ref[k:k + 1, cols] * ext[s, r0:r0 + ts, :]
                fs[s].append(f)
            tail[:, cols] = ext[nsub - 1, ts:ts + SUBLANES, :]
        for s in range(nsub):
            act[s, :, c * FC_TAIL:(c + 1) * FC_TAIL] = (_gelu_tanh(fs[s][0]) * fs[s][1]).astype(BF16)

    for s, rows in enumerate(subs):
        h2 = h1[s] + _rms(_dot(act[s], wdn_ref[...]), nfpost_ref[...])
        e = _dot(p_ref[rows, :].astype(BF16), wple_ref[...]) * _sigmoid(_dot(h2.astype(BF16), wpg_ref[...]))
        o_ref[rows, :] = h2 + _rms(e, nple_ref[...])


def _tail(x2, ya, yb, gmix, p2, w_a2d, w_b2d, w_o, n_mix, n_fpre, w_up, conv_ffn, conv_b, w_down, n_fpost,
          w_ple, w_ple_gate, n_ple, seq):
    t, d = x2.shape
    d_ff = w_down.shape[0]
    tm = min(TM_TAIL, seq)
    tok = lambda w: pl.BlockSpec((tm, w), lambda i: (i, 0))
    row = lambda v: v.reshape(1, -1)
    b16 = lambda w: w.astype(BF16)
    consts = [b16(w_a2d), b16(w_b2d), b16(w_o), row(n_mix), row(n_fpre), b16(w_up), conv_ffn, row(conv_b),
              b16(w_down), row(n_fpost), b16(w_ple), b16(w_ple_gate), row(n_ple)]
    return pl.pallas_call(
        functools.partial(_tail_kernel, tm=tm, seq=seq, d_ff=d_ff),
        grid=(t // tm,),
        in_specs=[tok(d), tok(ya.shape[1]), tok(yb.shape[1]), tok(2 * d), tok(p2.shape[1])]
                 + [_resident(c.shape) for c in consts],
        out_specs=tok(d),
        out_shape=jax.ShapeDtypeStruct((t, d), F32),
        scratch_shapes=[pltpu.VMEM((NSUB_TAIL, tm // NSUB_TAIL + SUBLANES, FC_TAIL), F32),
                        pltpu.VMEM((SUBLANES, 2 * d_ff), F32),
                        pltpu.VMEM((NSUB_TAIL, tm // NSUB_TAIL, d_ff), BF16)],
        compiler_params=pltpu.CompilerParams(dimension_semantics=("arbitrary",), vmem_limit_bytes=VMEM_LIMIT),
        name="tail_ffn",
    )(x2, ya, yb, gmix, p2, *consts)


def _layer(h2, p2, batch, seq, norm_mix_pre, w_in, conv_qkv, a_log, dt_bias, gdn_norm,
           cmp_pos_k, cmp_w1_k, cmp_w2_k, cmp_pos_v, cmp_w1_v, cmp_w2_v,
           w_a2d, w_b2d, w_o, norm_mix_post, norm_ffn_pre, w_up, conv_ffn, conv_ffn_b,
           w_down, norm_ffn_post, w_ple, w_ple_gate, norm_ple_post):
    qkv, z, qb, cmp, kv4, gmix, small = _in_proj(h2, norm_mix_pre, _prep_w_in(w_in))
    ya = _gdn(qkv, z, small, conv_qkv, a_log, dt_bias, gdn_norm, batch, seq)
    kcv = _compress(cmp, cmp_pos_k, cmp_w1_k, cmp_w2_k, cmp_pos_v, cmp_w1_v, cmp_w2_v, batch, seq)
    yb = _nsa_attention(qb, kcv, kv4, small, batch, seq)
    return _tail(h2, ya, yb, gmix, p2, w_a2d, w_b2d, w_o, norm_mix_post, norm_ffn_pre, w_up, conv_ffn,
                 conv_ffn_b, w_down, norm_ffn_post, w_ple, w_ple_gate, norm_ple_post, seq)


def kernel(x, p, norm_mix_pre, w_in, conv_qkv, a_log, dt_bias, gdn_norm, cmp_pos_k, cmp_w1_k, cmp_w2_k, cmp_pos_v, cmp_w1_v, cmp_w2_v, w_a2d, w_b2d, w_o, norm_mix_post, norm_ffn_pre, w_up, conv_ffn, conv_ffn_b, w_down, norm_ffn_post, w_ple, w_ple_gate, norm_ple_post):
    batch, seq, d = x.shape
    params = (norm_mix_pre, w_in, conv_qkv, a_log, dt_bias, gdn_norm, cmp_pos_k, cmp_w1_k, cmp_w2_k,
              cmp_pos_v, cmp_w1_v, cmp_w2_v, w_a2d, w_b2d, w_o, norm_mix_post, norm_ffn_pre, w_up,
              conv_ffn, conv_ffn_b, w_down, norm_ffn_post, w_ple, w_ple_gate, norm_ple_post)
    h = x.reshape(batch * seq, d)
    for i in range(p.shape[0]):
        h = _layer(h, p[i].reshape(batch * seq, -1), batch, seq, *[w[i] for w in params])
    return h.reshape(batch, seq, d)
```

```python
import functools

import numpy as np
import jax
import jax.numpy as jnp
from jax import lax
from jax.experimental import pallas as pl
from jax.experimental.pallas import tpu as pltpu

F32 = jnp.float32
BF16 = jnp.bfloat16
EPS = 1e-6
HIGHEST = lax.Precision.HIGHEST

GDN_HEADS, GDN_DK, GDN_DV, GDN_CONV, GDN_CHUNK = 4, 128, 128, 4, 64
NSA_HEADS, NSA_GROUPS, NSA_DK, NSA_DV = 8, 2, 64, 64
NSA_HPG = NSA_HEADS // NSA_GROUPS
CMP_LEN, CMP_STRIDE, CMP_HIDDEN = 32, 16, 256
SEL_BLOCK, SEL_TOPN = 64, 16
WINDOW = 512
FORCED_SCORE = 1e9
FFN_CONV = 3
IN_SPLITS = (512, 512, 512, 512, 4, 4, 512, 128, 128, 128, 128, 128, 128, 24, 2048)

LANES = 128
SUBLANES = 8
VMEM_LIMIT = 56 * 1024 * 1024

TM_PROJ = 512
TC_GDN = 256
TQ_NSA = 256
TM_TAIL = 512
NSUB_TAIL = 2
FC_TAIL = 256


def _log2(n):
    assert n & (n - 1) == 0
    return n.bit_length() - 1


def _dot(a, b, precision=None):
    return jnp.dot(a, b, preferred_element_type=F32, precision=precision)


def _dot_nt(a, b):
    return lax.dot_general(a, b, (((1,), (1,)), ((), ())), preferred_element_type=F32)


def _rms(x, w):
    return x * lax.rsqrt(jnp.mean(x * x, axis=-1, keepdims=True) + EPS) * w


def _sigmoid(x):
    return 1.0 / (1.0 + jnp.exp(-x))


def _silu(x):
    return x * _sigmoid(x)


def _gelu_tanh(x):
    c = 0.7978845608028654
    half = 0.5 * x
    return half + half * jnp.tanh(x * (c + (c * 0.044715) * (x * x)))


def _softplus(x):
    return jnp.maximum(x, 0.0) + jnp.log1p(jnp.exp(-jnp.abs(x)))


def _resident(shape):
    zeros = (0,) * len(shape)
    return pl.BlockSpec(shape, lambda *_: zeros, pipeline_mode=pl.Buffered(1))


PROJ_WIDTHS = (1536, 512, 512, 256, 512, 2048, 128)
PROJ_DTYPES = (F32, F32, BF16, BF16, BF16, BF16, F32)
PROJ_GATE = (False, False, False, False, False, True, False)
PROJ_COLS = 512


def _in_proj_kernel(x_ref, nw_ref, w_ref, *out_refs):
    x = x_ref[...]
    u = _rms(x, nw_ref[...]).astype(BF16)
    off = 0
    for o_ref, width, gate in zip(out_refs, PROJ_WIDTHS, PROJ_GATE):
        for c0 in range(0, width, PROJ_COLS):
            c1 = min(c0 + PROJ_COLS, width)
            y = _dot(u, w_ref[:, off + c0:off + c1])
            o_ref[:, c0:c1] = (_sigmoid(y) if gate else y).astype(o_ref.dtype)
        off += width


def _prep_w_in(w_in):
    offs = np.cumsum((0,) + IN_SPLITS)
    w16 = w_in.astype(BF16)
    qa, ka, va, za, ba, aa, qb, kcm, vcm, ksl, vsl, kwi, vwi, gnsa, gmix = [
        w16[:, offs[i]:offs[i + 1]] for i in range(len(IN_SPLITS))]
    qb = qb * (NSA_DK ** -0.5)
    grp = lambda y, g: y[:, g * NSA_DK:(g + 1) * NSA_DK]
    kv4 = jnp.concatenate([grp(ksl, 0), grp(vsl, 0), grp(ksl, 1), grp(vsl, 1),
                           grp(kwi, 0), grp(vwi, 0), grp(kwi, 1), grp(vwi, 1)], axis=1)
    used = 2 * GDN_HEADS + NSA_HEADS * 3
    small = jnp.concatenate([ba, aa, gnsa, jnp.zeros((w_in.shape[0], LANES - used), BF16)], axis=1)
    w = jnp.concatenate([qa, ka, va, za, qb, kcm, vcm, kv4, gmix, small], axis=1)
    assert w.shape[1] == sum(PROJ_WIDTHS)
    return w


def _in_proj(x2, norm_w, w_prep):
    t, d = x2.shape
    tm = min(TM_PROJ, t)
    n = w_prep.shape[1]
    return pl.pallas_call(
        _in_proj_kernel,
        grid=(t // tm,),
        in_specs=[pl.BlockSpec((tm, d), lambda i: (i, 0)),
                  _resident((1, d)),
                  _resident((d, n))],
        out_specs=[pl.BlockSpec((tm, wd), lambda i: (i, 0)) for wd in PROJ_WIDTHS],
        out_shape=[jax.ShapeDtypeStruct((t, wd), dt) for wd, dt in zip(PROJ_WIDTHS, PROJ_DTYPES)],
        compiler_params=pltpu.CompilerParams(dimension_semantics=("arbitrary",), vmem_limit_bytes=VMEM_LIMIT),
        name="in_proj",
    )(x2, norm_w.reshape(1, d), w_prep)


def _gdn_kernel(qkv_ref, z_ref, sm_ref, cw_ref, alog_ref, dtb_ref, nw_ref, o_ref,
                xbuf, qn, kn, vn, gcs, bts, state, *, tc):
    H, DK, C = GDN_HEADS, GDN_DK, GDN_CHUNK

    @pl.when(pl.program_id(1) == 0)
    def _():
        xbuf[0:SUBLANES, :] = jnp.zeros((SUBLANES, xbuf.shape[1]), F32)
        state[...] = jnp.zeros(state.shape, F32)

    xbuf[SUBLANES:SUBLANES + tc, :] = qkv_ref[...]

    for blk in range(3 * H):
        cs = slice(blk * DK, (blk + 1) * DK)
        conv = cw_ref[GDN_CONV - 1:GDN_CONV, cs] * xbuf[SUBLANES:SUBLANES + tc, cs]
        for kk in range(GDN_CONV - 1):
            r0 = SUBLANES - (GDN_CONV - 1) + kk
            conv = conv + cw_ref[kk:kk + 1, cs] * xbuf[r0:r0 + tc, cs]
        act = _silu(conv)
        which, h = divmod(blk, H)
        if which == 0:
            qn[h] = act * lax.rsqrt(jnp.sum(act * act, axis=-1, keepdims=True) + EPS) * (DK ** -0.5)
        elif which == 1:
            kn[h] = act * lax.rsqrt(jnp.sum(act * act, axis=-1, keepdims=True) + EPS)
        else:
            vn[h] = act
    xbuf[0:SUBLANES, :] = xbuf[tc:tc + SUBLANES, :]

    sm = sm_ref[...]
    bts[...] = _sigmoid(sm)
    g_all = -jnp.exp(alog_ref[...]) * _softplus(sm + dtb_ref[...])
    ri = lax.broadcasted_iota(jnp.int32, (tc, tc), 0)
    ci = lax.broadcasted_iota(jnp.int32, (tc, tc), 1)
    same_chunk = (ri >> _log2(C)) == (ci >> _log2(C))
    block_tril = jnp.where(same_chunk & (ri >= ci), 1.0, 0.0).astype(F32)
    gcs[...] = _dot(block_tril, g_all, precision=HIGHEST)

    i64 = lax.broadcasted_iota(jnp.int32, (C, C), 0)
    j64 = lax.broadcasted_iota(jnp.int32, (C, C), 1)
    incl = i64 >= j64
    strict = i64 > j64
    eye = jnp.where(i64 == j64, 1.0, 0.0).astype(F32)
    pair_masks = [((i64 >> (lv + 1)) == (j64 >> (lv + 1))) & ((i64 >> lv) != (j64 >> lv)) for lv in range(_log2(C))]
    nw = nw_ref[...]

    units = [(c, h) for c in range(tc // C) for h in range(H)]
    rows = [slice(c * C, (c + 1) * C) for c in range(tc // C)]
    gc = [gcs[r, :] for r in rows]
    bt = [bts[r, :] for r in rows]
    gc_t = [x.T for x in gc]
    bt_t = [x.T for x in bt]
    gc_col = [gc[c][:, H + h:H + h + 1] for c, h in units]
    g_last = [gc[c][C - 1:C, H + h:H + h + 1] for c, h in units]
    e = [jnp.exp(jnp.where(incl, gc_col[n] - gc_t[c][H + h:H + h + 1, :], 0.0)) for n, (c, h) in enumerate(units)]
    q = [qn[h, rows[c], :] for c, h in units]
    k = [kn[h, rows[c], :] for c, h in units]
    kb = [x.astype(BF16) for x in k]
    kkt = [_dot_nt(x, x) for x in kb]
    qkm = [(_dot_nt(q[n].astype(BF16), kb[n]) * jnp.where(incl, e[n], 0.0)).astype(BF16) for n in range(len(units))]
    low = [bt[c][:, h:h + 1] * kkt[n] * jnp.where(strict, e[n], 0.0) for n, (c, h) in enumerate(units)]
    acc = [-jnp.where(pair_masks[0], x, 0.0) for x in low]
    for pm in pair_masks[1:]:
        off = [jnp.where(pm, x, 0.0) for x in low]
        tb = [off[n] + _dot(acc[n].astype(BF16), off[n].astype(BF16)) for n in range(len(units))]
        acc = [acc[n] - tb[n] - _dot(tb[n].astype(BF16), acc[n].astype(BF16)) for n in range(len(units))]
    t_beta = [((eye + acc[n]) * bt_t[c][h:h + 1, :]).astype(BF16) for n, (c, h) in enumerate(units)]
    gam = [jnp.exp(x) for x in gc_col]
    sol = [_dot(t_beta[n], jnp.concatenate([k[n] * gam[n], vn[h, rows[c], :]], axis=1).astype(BF16))
           for n, (c, h) in enumerate(units)]
    wq = [jnp.concatenate([sol[n][:, :DK], q[n] * gam[n]], axis=0).astype(BF16) for n in range(len(units))]
    kt_t = [(k[n] * jnp.exp(g_last[n] - gc_col[n])).T.astype(BF16) for n in range(len(units))]
    s_decay = [jnp.exp(x) for x in g_last]

    heads = range(H)
    s_cur = [state[h] for h in heads]
    for c in range(tc // C):
        ns = [c * H + h for h in heads]
        ws = [_dot(wq[n], s_cur[h].astype(BF16)) for h, n in zip(heads, ns)]
        ub = [(sol[n][:, DK:] - ws[h][:C]).astype(BF16) for h, n in zip(heads, ns)]
        o = [ws[h][C:] + _dot(qkm[n], ub[h]) for h, n in zip(heads, ns)]
        s_cur = [s_decay[n] * s_cur[h] + _dot(kt_t[n], ub[h]) for h, n in zip(heads, ns)]
        for h in heads:
            y = _rms(o[h], nw) * _silu(z_ref[rows[c], h * GDN_DV:(h + 1) * GDN_DV])
            o_ref[rows[c], h * GDN_DV:(h + 1) * GDN_DV] = y.astype(o_ref.dtype)
    for h in heads:
        state[h] = s_cur[h]


def _gdn(qkv, z, small_a, conv_w, a_log, dt_bias, norm_w, batch, seq):
    t = qkv.shape[0]
    tc = min(TC_GDN, seq)
    nj = seq // tc
    H = GDN_HEADS
    pad = lambda v: jnp.zeros((1, LANES), F32).at[0, H:2 * H].set(v.astype(F32))
    tok = lambda w: pl.BlockSpec((tc, w), lambda b, j: (b * nj + j, 0))
    return pl.pallas_call(
        functools.partial(_gdn_kernel, tc=tc),
        grid=(batch, nj),
        in_specs=[tok(3 * H * GDN_DK), tok(H * GDN_DV), tok(LANES),
                  _resident(conv_w.shape), _resident((1, LANES)), _resident((1, LANES)),
                  _resident((1, GDN_DV))],
        out_specs=tok(H * GDN_DV),
        out_shape=jax.ShapeDtypeStruct((t, H * GDN_DV), BF16),
        scratch_shapes=[pltpu.VMEM((tc + SUBLANES, 3 * H * GDN_DK), F32),
                        pltpu.VMEM((H, tc, GDN_DK), F32),
                        pltpu.VMEM((H, tc, GDN_DK), F32),
                        pltpu.VMEM((H, tc, GDN_DV), F32),
                        pltpu.VMEM((tc, LANES), F32),
                        pltpu.VMEM((tc, LANES), F32),
                        pltpu.VMEM((H, GDN_DK, GDN_DV), F32)],
        compiler_params=pltpu.CompilerParams(dimension_semantics=("arbitrary", "arbitrary"),
                                             vmem_limit_bytes=VMEM_LIMIT),
        name="gdn_mixer",
    )(qkv, z, small_a, conv_w, pad(a_log), pad(dt_bias), norm_w.reshape(1, GDN_DV))


def _compress_kernel(x_ref, pos_ref, w1_ref, w2_ref, o_ref, shift, *, nsub):
    half = CMP_STRIDE * NSA_DK
    x = x_ref[0, 0]
    pos = jnp.broadcast_to(pos_ref[0], (SUBLANES, 2 * half)).astype(BF16)
    first = _dot(x, w1_ref[0, :half, :]) + _dot(pos[:, :half], w1_ref[0, :half, :])[0:1]
    second = _dot(x, w1_ref[0, half:, :]) + _dot(pos[:, half:], w1_ref[0, half:, :])[0:1]
    shift[0:nsub, :] = second
    shift[nsub:nsub + SUBLANES, :] = jnp.zeros((SUBLANES, CMP_HIDDEN), F32)
    hidden = first + shift[1:nsub + 1, :]
    y = _dot(_gelu_tanh(hidden).astype(BF16), w2_ref[0])
    row = lax.broadcasted_iota(jnp.int32, y.shape, 0)
    o_ref[0, 0] = jnp.where(row < nsub - 1, y, 0.0)


def _compress(cmp, pos_k, w1_k, w2_k, pos_v, w1_v, w2_v, batch, seq):
    G, d = NSA_GROUPS, NSA_DK
    nsub = seq // CMP_STRIDE
    xs = cmp.reshape(batch, nsub, CMP_STRIDE, 2 * G, d).transpose(0, 3, 1, 2, 4).reshape(batch, 2 * G, nsub, CMP_STRIDE * d)
    pos = jnp.stack([pos_k.reshape(1, -1), pos_v.reshape(1, -1)])
    w1 = jnp.stack([w1_k, w1_v]).astype(BF16)
    w2 = jnp.stack([w2_k, w2_v]).astype(BF16)
    out = pl.pallas_call(
        functools.partial(_compress_kernel, nsub=nsub),
        grid=(batch, 2 * G),
        in_specs=[pl.BlockSpec((1, 1, nsub, CMP_STRIDE * d), lambda b, c: (b, c, 0, 0)),
                  pl.BlockSpec((1, 1, CMP_LEN * d), lambda b, c: (c // G, 0, 0)),
                  pl.BlockSpec((1, CMP_LEN * d, CMP_HIDDEN), lambda b, c: (c // G, 0, 0)),
                  pl.BlockSpec((1, CMP_HIDDEN, d), lambda b, c: (c // G, 0, 0))],
        out_specs=pl.BlockSpec((1, 1, nsub, d), lambda b, c: (b, c, 0, 0)),
        out_shape=jax.ShapeDtypeStruct((batch, 2 * G, nsub, d), F32),
        scratch_shapes=[pltpu.VMEM((nsub + SUBLANES, CMP_HIDDEN), F32)],
        compiler_params=pltpu.CompilerParams(dimension_semantics=("arbitrary", "arbitrary"),
                                             vmem_limit_bytes=VMEM_LIMIT),
        name="nsa_compress",
    )(xs, pos, w1, w2)
    return jnp.concatenate([out[:, :G], out[:, G:]], axis=-1).astype(BF16)


NSA_MASK = -1e30
NSA_M0 = -1e29
NSA_VROWS = NSA_DV + 16
SEL, WIN = 0, 1
NSA_ONEHOT_ROW = 80
NSA_BIAS_ROWS = 16


def _nsa_kernel(q_ref, kcv_ref, ksel_ref, kwin_ref, gate_ref, slope_ref, o_ref,
                kc_aug, vc_t, k_all, v_all, q_sc, sbuf, negq, m_all, acc_all, *, tq, seq):
    R, DK, DV = NSA_HPG, NSA_DK, NSA_DV
    nsub = seq // CMP_STRIDE
    nsel = seq // SEL_BLOCK
    topn = min(SEL_TOPN, nsel)
    nql = R * tq
    per_step = tq // SEL_BLOCK
    nstep_sel = nsel // per_step
    pad = WINDOW // tq
    i = pl.program_id(2)
    t0 = i * tq
    d_kq = (lax.broadcasted_iota(jnp.int32, (tq, nql), 0)
            - (lax.broadcasted_iota(jnp.int32, (tq, nql), 1) & (tq - 1)))

    def aug_keys(kv, pos):
        lane = lax.broadcasted_iota(jnp.int32, kv.shape, 1)
        block_in_step = (pos >> _log2(SEL_BLOCK)) & (per_step - 1)
        onehot = jnp.where(lane - NSA_ONEHOT_ROW == block_in_step, 1.0, 0.0)
        aug = jnp.where(lane == DK, (pos >> _log2(LANES)).astype(F32),
                        jnp.where(lane == DK + 1, (pos & (LANES - 1)).astype(F32),
                                  jnp.where(lane == DK + 2, 1.0, onehot)))
        return jnp.where(lane < DK, kv, aug).astype(BF16)

    def values_t(kv):
        t = kv.T
        return jnp.concatenate([t[DK:DK + DV], jnp.ones((NSA_VROWS - DV, kv.shape[0]), F32)], axis=0).astype(BF16)

    @pl.when(i == 0)
    def _():
        lane = lax.broadcasted_iota(jnp.int32, (pad * tq, LANES), 1)
        pad_keys = jnp.where(lane == DK, NSA_MASK, 0.0).astype(BF16)
        for br, src in ((SEL, ksel_ref), (WIN, kwin_ref)):
            k_all[br, 0:pad * tq, :] = pad_keys
            v_all[br, 0:pad] = jnp.zeros((pad, NSA_VROWS, tq), BF16)

        def fill(c, carry):
            r0 = pl.multiple_of(c * tq, tq)
            pos = lax.broadcasted_iota(jnp.int32, (tq, LANES), 0) + r0
            for br, src in ((SEL, ksel_ref), (WIN, kwin_ref)):
                kv = src[pl.ds(r0, tq), :].astype(F32)
                k_all[br, pl.ds(r0 + pad * tq, tq), :] = aug_keys(kv, pos)
                v_all[br, c + pad] = values_t(kv)
            return carry

        lax.fori_loop(0, seq // tq, fill, 0)
        kcv = kcv_ref[0, 0].astype(F32)
        pos_c = lax.broadcasted_iota(jnp.int32, (nsub, LANES), 0) * CMP_STRIDE + (CMP_LEN - 1)
        kc_aug[...] = aug_keys(kcv, pos_c)
        vc_t[...] = kcv.T[DK:DK + DV].astype(BF16)
        negq[nstep_sel] = jnp.zeros((NSA_BIAS_ROWS, nql), BF16)

    row_q = lax.broadcasted_iota(jnp.int32, (LANES - DK, tq), 0)
    t0_f = (jnp.zeros((LANES - DK, tq), jnp.int32) + t0).astype(F32)
    q_all = q_ref[...].astype(F32).T
    qts = []
    for r in range(R):
        sl = slope_ref[0, r:r + 1, 0:1]
        aug = jnp.where(row_q == 0, sl * float(LANES),
                        jnp.where(row_q == 1, sl, jnp.where(row_q == 2, -sl * t0_f, 0.0)))
        qts.append(jnp.concatenate([q_all[r * DK:(r + 1) * DK], aug], axis=0))
    q_t = jnp.concatenate(qts, axis=1).astype(BF16)

    blk_end = lax.broadcasted_iota(jnp.int32, (nsub, nql), 0) * CMP_STRIDE + (CMP_LEN - 1)
    q_pos = (lax.broadcasted_iota(jnp.int32, (nsub, nql), 1) & (tq - 1)) + t0
    s = jnp.where(blk_end <= q_pos, _dot(kc_aug[...], q_t), NSA_MASK)
    m = jnp.maximum(jnp.max(s, axis=0, keepdims=True), NSA_M0)
    e = jnp.exp(s - m)
    den = jnp.sum(e, axis=0, keepdims=True)
    p = e * jnp.where(den > 0.0, 1.0 / den, 0.0)
    o_cmp = _dot(vc_t[...], p.astype(BF16))
    psum = p[:, 0:tq]
    for r in range(1, R):
        psum = psum + p[:, r * tq:(r + 1) * tq]

    nwin = WINDOW // tq + 1
    oh = NSA_ONEHOT_ROW

    def keys_of(br, step):
        return k_all[br, pl.ds(pl.multiple_of(step * tq, tq), tq), :]

    def accumulate(s, br, step):
        m_old = m_all[br]
        m_new = jnp.maximum(m_old, jnp.max(s, axis=0, keepdims=True))
        p = jnp.exp(s - m_new).astype(BF16)
        acc_all[br] = jnp.exp(m_old - m_new) * acc_all[br] + _dot(v_all[br, step], p)
        m_all[br] = m_new

    m_all[...] = jnp.full(m_all.shape, NSA_M0, F32)
    acc_all[...] = jnp.zeros(acc_all.shape, F32)

    s_win = [_dot(keys_of(WIN, i + w), q_t) for w in range(nwin)]
    s_win[0] = jnp.where(d_kq > 0, s_win[0], NSA_MASK)
    s_win[-1] = jnp.where(d_kq <= 0, s_win[-1], NSA_MASK)
    m_win = jnp.max(s_win[0], axis=0, keepdims=True)
    for s_w in s_win[1:]:
        m_win = jnp.maximum(m_win, jnp.max(s_w, axis=0, keepdims=True))
    pv_win = []

    def window_piece(w):
        pv_win.append(_dot(v_all[WIN, i + w], jnp.exp(s_win[w] - m_win).astype(BF16)))

    j_o = lax.broadcasted_iota(jnp.int32, (nsel, nsub), 0) * SEL_BLOCK
    n_o = lax.broadcasted_iota(jnp.int32, (nsel, nsub), 1) * CMP_STRIDE
    ov = jnp.maximum(jnp.minimum(n_o + CMP_LEN, j_o + SEL_BLOCK) - jnp.maximum(n_o, j_o), 0)
    ov = (ov.astype(F32) * (1.0 / CMP_LEN)).astype(BF16)
    p_hi = psum.astype(BF16)
    p_lo = (psum - p_hi.astype(F32)).astype(BF16)
    imp = _dot(ov, p_hi) + _dot(ov, p_lo)
    blk = lax.broadcasted_iota(jnp.int32, (nsel, tq), 0)
    cur = (lax.broadcasted_iota(jnp.int32, (nsel, tq), 1) + t0) >> _log2(SEL_BLOCK)
    valid_s = blk <= cur
    forced = (blk == 0) | (blk == cur) | (blk == cur - 1)
    score = jnp.where(valid_s, jnp.where(forced, FORCED_SCORE, imp), -jnp.inf)
    blk_f = blk.astype(F32)
    npieces = nwin
    done = 0
    for it in range(topn):
        m = jnp.max(score, axis=0, keepdims=True)
        first = jnp.min(jnp.where(score == m, blk_f, float(nsel)), axis=0, keepdims=True)
        score = jnp.where(blk_f == first, -jnp.inf, score)
        while done < ((it + 1) * npieces) // topn:
            window_piece(done)
            done += 1
    acc_all[WIN] = functools.reduce(lambda a, b: a + b, pv_win)
    m_all[WIN] = m_win
    nb = jnp.where(valid_s & (score == -jnp.inf), 0.0, NSA_MASK)
    for n in range(nstep_sel):
        rows = jnp.concatenate([nb[n * per_step:(n + 1) * per_step],
                                jnp.zeros((NSA_BIAS_ROWS - per_step, tq), F32)], axis=0)
        negq[n] = jnp.concatenate([rows] * R, axis=1).astype(BF16)

    q_diag = jnp.concatenate([q_t[:oh], negq[i], q_t[oh + NSA_BIAS_ROWS:]], axis=0)
    accumulate(jnp.where(d_kq <= 0, _dot(keys_of(SEL, i + pad), q_diag), NSA_MASK), SEL, i + pad)

    def schedule(n):
        dummy = n >= i
        return jnp.where(dummy, WIN, SEL), jnp.where(dummy, 0, n + pad), jnp.where(dummy, nstep_sel, n)

    def issue_scores(n, slot):
        br, step, group = schedule(n)
        q_sc[oh:oh + NSA_BIAS_ROWS, :] = negq[group]
        sbuf[slot] = _dot(keys_of(br, step), q_sc[...])

    def consume(n, slot):
        br, step, _ = schedule(n)
        accumulate(sbuf[slot], br, step)

    q_sc[...] = q_t
    issue_scores(0, 0)

    def trip(n, steps):
        for j in range(steps):
            issue_scores(n + j + 1, (j + 1) & 1)
            consume(n + j, j & 1)

    def loop(count, first, steps):
        def body(m, carry):
            trip(first + steps * m, steps)
            return carry
        lax.fori_loop(0, count, body, 0)

    long_trips = i >> 2
    loop(long_trips, 0, 4)
    loop((i - 4 * long_trips + 1) >> 1, 4 * long_trips, 2)
    acc_s = acc_all[SEL]
    acc_w = acc_all[WIN]
    o_sel = acc_s[:DV] / acc_s[DV:DV + 1]
    o_win = acc_w[:DV] / acc_w[DV:DV + 1]

    g_all = _sigmoid(gate_ref[...]).T
    g0 = 2 * GDN_HEADS
    g_t = jnp.where(pl.program_id(1) == 0, g_all[g0:g0 + 3 * R], g_all[g0 + 3 * R:g0 + 6 * R])
    ys = []
    for r in range(R):
        cs = slice(r * tq, (r + 1) * tq)
        ys.append(g_t[3 * r:3 * r + 1] * o_cmp[:, cs] + g_t[3 * r + 1:3 * r + 2] * o_sel[:, cs]
                  + g_t[3 * r + 2:3 * r + 3] * o_win[:, cs])
    for half in range(R // 2):
        pair = jnp.concatenate([ys[2 * half], ys[2 * half + 1]], axis=0)
        o_ref[:, half * LANES:(half + 1) * LANES] = pair.T.astype(o_ref.dtype)


def _nsa_attention(qb, kcv, kv4, small, batch, seq):
    t = qb.shape[0]
    G, R = NSA_GROUPS, NSA_HPG
    tq = TQ_NSA
    assert tq % LANES == 0 and WINDOW % tq == 0 and seq % tq == 0 and LANES == 2 * NSA_DK
    nq = seq // tq
    nsub = seq // CMP_STRIDE
    nsel = seq // SEL_BLOCK
    slopes = 2.0 ** (-8.0 * jnp.arange(1, NSA_HEADS + 1, dtype=F32) / NSA_HEADS)
    slope_rows = jnp.zeros((G, SUBLANES, LANES), F32).at[:, :R, :].set(
        jnp.broadcast_to(slopes.reshape(G, R, 1), (G, R, LANES)))
    return pl.pallas_call(
        functools.partial(_nsa_kernel, tq=tq, seq=seq),
        grid=(batch, G, nq),
        in_specs=[pl.BlockSpec((tq, R * NSA_DK), lambda b, g, i: (b * nq + i, g)),
                  pl.BlockSpec((1, 1, nsub, LANES), lambda b, g, i: (b, g, 0, 0)),
                  pl.BlockSpec((seq, LANES), lambda b, g, i: (b, g)),
                  pl.BlockSpec((seq, LANES), lambda b, g, i: (b, G + g)),
                  pl.BlockSpec((tq, LANES), lambda b, g, i: (b * nq + i, 0)),
                  pl.BlockSpec((1, SUBLANES, LANES), lambda b, g, i: (g, 0, 0))],
        out_specs=pl.BlockSpec((tq, R * NSA_DV), lambda b, g, i: (b * nq + i, g)),
        out_shape=jax.ShapeDtypeStruct((t, NSA_HEADS * NSA_DV), BF16),
        scratch_shapes=[pltpu.VMEM((nsub, LANES), BF16),
                        pltpu.VMEM((NSA_DV, nsub), BF16),
                        pltpu.VMEM((2, seq + WINDOW, LANES), BF16),
                        pltpu.VMEM((2, (seq + WINDOW) // tq, NSA_VROWS, tq), BF16),
                        pltpu.VMEM((LANES, R * tq), BF16),
                        pltpu.VMEM((2, tq, R * tq), F32),
                        pltpu.VMEM((nsel // (tq // SEL_BLOCK) + 1, NSA_BIAS_ROWS, R * tq), BF16),
                        pltpu.VMEM((2, 1, R * tq), F32),
                        pltpu.VMEM((2, NSA_VROWS, R * tq), F32)],
        compiler_params=pltpu.CompilerParams(dimension_semantics=("arbitrary", "arbitrary", "arbitrary"),
                                             vmem_limit_bytes=VMEM_LIMIT),
        name="nsa_attention",
    )(qb, kcv, kv4, kv4, small, slope_rows)


def _tail_kernel(x_ref, ya_ref, yb_ref, gm_ref, p_ref, wa_ref, wb_ref, wo_ref, nmix_ref, nfpre_ref,
                 wup_ref, cw_ref, cb_ref, wdn_ref, nfpost_ref, wple_ref, wpg_ref, nple_ref, o_ref,
                 ext, tail, act, *, tm, seq, d_ff):
    d = x_ref.shape[1]
    halo = FFN_CONV - 1

    @pl.when((pl.program_id(0) * tm) % seq == 0)
    def _():
        tail[...] = jnp.zeros(tail.shape, F32)

    nsub = ext.shape[0]
    ts = tm // nsub
    subs = [slice(s * ts, (s + 1) * ts) for s in range(nsub)]
    h1, u = [], []
    for rows in subs:
        mixed = (gm_ref[rows, :d].astype(F32) * _dot(ya_ref[rows, :], wa_ref[...])
                 + gm_ref[rows, d:].astype(F32) * _dot(yb_ref[rows, :], wb_ref[...]))
        h1.append(x_ref[rows, :] + _rms(_dot(mixed.astype(BF16), wo_ref[...]), nmix_ref[...]))
        u.append(_rms(h1[-1], nfpre_ref[...]).astype(BF16))

    for c in range(d_ff // FC_TAIL):
        fs = [[] for _ in range(nsub)]
        for part in range(2):
            cols = slice(part * d_ff + c * FC_TAIL, part * d_ff + (c + 1) * FC_TAIL)
            for s in range(nsub):
                ext[s, 0:SUBLANES, :] = tail[:, cols] if s == 0 else ext[s - 1, ts:ts + SUBLANES, :]
                ext[s, SUBLANES:SUBLANES + ts, :] = _dot(u[s], wup_ref[:, cols])
                f = cb_ref[:, cols] + cw_ref[halo:halo + 1, cols] * ext[s, SUBLANES:SUBLANES + ts, :]
                for k in range(halo):
                    r0 = SUBLANES - halo + k
                    f = f + cw_ref[k:k + 1, cols] * ext[s, r0:r0 + ts, :]
                fs[s].append(f)
            tail[:, cols] = ext[nsub - 1, ts:ts + SUBLANES, :]
        for s in range(nsub):
            act[s, :, c * FC_TAIL:(c + 1) * FC_TAIL] = (_gelu_tanh(fs[s][0]) * fs[s][1]).astype(BF16)

    for s, rows in enumerate(subs):
        h2 = h1[s] + _rms(_dot(act[s], wdn_ref[...]), nfpost_ref[...])
        e = _dot(p_ref[rows, :].astype(BF16), wple_ref[...]) * _sigmoid(_dot(h2.astype(BF16), wpg_ref[...]))
        o_ref[rows, :] = h2 + _rms(e, nple_ref[...])


def _tail(x2, ya, yb, gmix, p2, w_a2d, w_b2d, w_o, n_mix, n_fpre, w_up, conv_ffn, conv_b, w_down, n_fpost,
          w_ple, w_ple_gate, n_ple, seq):
    t, d = x2.shape
    d_ff = w_down.shape[0]
    tm = min(TM_TAIL, seq)
    tok = lambda w: pl.BlockSpec((tm, w), lambda i: (i, 0))
    row = lambda v: v.reshape(1, -1)
    b16 = lambda w: w.astype(BF16)
    consts = [b16(w_a2d), b16(w_b2d), b16(w_o), row(n_mix), row(n_fpre), b16(w_up), conv_ffn, row(conv_b),
              b16(w_down), row(n_fpost), b16(w_ple), b16(w_ple_gate), row(n_ple)]
    return pl.pallas_call(
        functools.partial(_tail_kernel, tm=tm, seq=seq, d_ff=d_ff),
        grid=(t // tm,),
        in_specs=[tok(d), tok(ya.shape[1]), tok(yb.shape[1]), tok(2 * d), tok(p2.shape[1])]
                 + [_resident(c.shape) for c in consts],
        out_specs=tok(d),
        out_shape=jax.ShapeDtypeStruct((t, d), F32),
        scratch_shapes=[pltpu.VMEM((NSUB_TAIL, tm // NSUB_TAIL + SUBLANES, FC_TAIL), F32),
                        pltpu.VMEM((SUBLANES, 2 * d_ff), F32),
                        pltpu.VMEM((NSUB_TAIL, tm // NSUB_TAIL, d_ff), BF16)],
        compiler_params=pltpu.CompilerParams(dimension_semantics=("arbitrary",), vmem_limit_bytes=VMEM_LIMIT),
        name="tail_ffn",
    )(x2, ya, yb, gmix, p2, *consts)


def _layer(h2, p2, batch, seq, norm_mix_pre, w_in, conv_qkv, a_log, dt_bias, gdn_norm,
           cmp_pos_k, cmp_w1_k, cmp_w2_k, cmp_pos_v, cmp_w1_v, cmp_w2_v,
           w_a2d, w_b2d, w_o, norm_mix_post, norm_ffn_pre, w_up, conv_ffn, conv_ffn_b,
           w_down, norm_ffn_post, w_ple, w_ple_gate, norm_ple_post):
    qkv, z, qb, cmp, kv4, gmix, small = _in_proj(h2, norm_mix_pre, _prep_w_in(w_in))
    ya = _gdn(qkv, z, small, conv_qkv, a_log, dt_bias, gdn_norm, batch, seq)
    kcv = _compress(cmp, cmp_pos_k, cmp_w1_k, cmp_w2_k, cmp_pos_v, cmp_w1_v, cmp_w2_v, batch, seq)
    yb = _nsa_attention(qb, kcv, kv4, small, batch, seq)
    return _tail(h2, ya, yb, gmix, p2, w_a2d, w_b2d, w_o, norm_mix_post, norm_ffn_pre, w_up, conv_ffn,
                 conv_ffn_b, w_down, norm_ffn_post, w_ple, w_ple_gate, norm_ple_post, seq)


def kernel(x, p, norm_mix_pre, w_in, conv_qkv, a_log, dt_bias, gdn_norm, cmp_pos_k, cmp_w1_k, cmp_w2_k, cmp_pos_v, cmp_w1_v, cmp_w2_v, w_a2d, w_b2d, w_o, norm_mix_post, norm_ffn_pre, w_up, conv_ffn, conv_ffn_b, w_down, norm_ffn_post, w_ple, w_ple_gate, norm_ple_post):
    batch, seq, d = x.shape
    params = (norm_mix_pre, w_in, conv_qkv, a_log, dt_bias, gdn_norm, cmp_pos_k, cmp_w1_k, cmp_w2_k,
              cmp_pos_v, cmp_w1_v, cmp_w2_v, w_a2d, w_b2d, w_o, norm_mix_post, norm_ffn_pre, w_up,
              conv_ffn, conv_ffn_b, w_down, norm_ffn_post, w_ple, w_ple_gate, norm_ple_post)
    h = x.reshape(batch * seq, d)
    for i in range(p.shape[0]):
        h = _layer(h, p[i].reshape(batch * seq, -1), batch, seq, *[w[i] for w in params])
    return h.reshape(batch, seq, d)
```

```python
import functools

import numpy as np
import jax
import jax.numpy as jnp
from jax import lax
from jax.experimental import pallas as pl
from jax.experimental.pallas import tpu as pltpu

F32 = jnp.float32
BF16 = jnp.bfloat16
EPS = 1e-6
HIGHEST = lax.Precision.HIGHEST

GDN_HEADS, GDN_DK, GDN_DV, GDN_CONV, GDN_CHUNK = 4, 128, 128, 4, 64
NSA_HEADS, NSA_GROUPS, NSA_DK, NSA_DV = 8, 2, 64, 64
NSA_HPG = NSA_HEADS // NSA_GROUPS
CMP_LEN, CMP_STRIDE, CMP_HIDDEN = 32, 16, 256
SEL_BLOCK, SEL_TOPN = 64, 16
WINDOW = 512
FORCED_SCORE = 1e9
FFN_CONV = 3
IN_SPLITS = (512, 512, 512, 512, 4, 4, 512, 128, 128, 128, 128, 128, 128, 24, 2048)

LANES = 128
SUBLANES = 8
VMEM_LIMIT = 56 * 1024 * 1024

TM_PROJ = 512
TC_GDN = 256
TQ_NSA = 256
TM_TAIL = 512
NSUB_TAIL = 2
FC_TAIL = 256


def _log2(n):
    assert n & (n - 1) == 0
    return n.bit_length() - 1


def _dot(a, b, precision=None):
    return jnp.dot(a, b, preferred_element_type=F32, precision=precision)


def _dot_nt(a, b):
    return lax.dot_general(a, b, (((1,), (1,)), ((), ())), preferred_element_type=F32)


def _rms(x, w):
    return x * lax.rsqrt(jnp.mean(x * x, axis=-1, keepdims=True) + EPS) * w


def _sigmoid(x):
    return 1.0 / (1.0 + jnp.exp(-x))


def _silu(x):
    return x * _sigmoid(x)


def _gelu_tanh(x):
    c = 0.7978845608028654
    half = 0.5 * x
    return half + half * jnp.tanh(x * (c + (c * 0.044715) * (x * x)))


def _softplus(x):
    return jnp.maximum(x, 0.0) + jnp.log1p(jnp.exp(-jnp.abs(x)))


def _resident(shape):
    zeros = (0,) * len(shape)
    return pl.BlockSpec(shape, lambda *_: zeros, pipeline_mode=pl.Buffered(1))


PROJ_WIDTHS = (1536, 512, 512, 256, 512, 2048, 128)
PROJ_DTYPES = (F32, F32, BF16, BF16, BF16, BF16, F32)
PROJ_GATE = (False, False, False, False, False, True, False)
PROJ_COLS = 512
PROJ_CMP = 3


def _in_proj_kernel(x_ref, nw_ref, w_ref, *refs):
    out_refs, cmp_sc = refs[:-1], refs[-1]
    x = x_ref[...]
    u = _rms(x, nw_ref[...]).astype(BF16)
    off = 0
    for idx, (o_ref, width, gate) in enumerate(zip(out_refs, PROJ_WIDTHS, PROJ_GATE)):
        if idx == PROJ_CMP:
            y = _dot(u, w_ref[:, off:off + width])
            nblk = o_ref.shape[2]
            per_lane_group = LANES // NSA_DK
            for h in range(width // LANES):
                cmp_sc[h] = y[:, h * LANES:(h + 1) * LANES]
                for l in range(CMP_STRIDE):
                    rows = cmp_sc[h, pl.ds(l, nblk, stride=CMP_STRIDE), :]
                    for k in range(per_lane_group):
                        piece = rows[:, k * NSA_DK:(k + 1) * NSA_DK]
                        o_ref[0, h * per_lane_group + k, :, l * NSA_DK:(l + 1) * NSA_DK] = piece.astype(o_ref.dtype)
            off += width
            continue
        for c0 in range(0, width, PROJ_COLS):
            c1 = min(c0 + PROJ_COLS, width)
            y = _dot(u, w_ref[:, off + c0:off + c1])
            o_ref[:, c0:c1] = (_sigmoid(y) if gate else y).astype(o_ref.dtype)
        off += width


def _prep_w_in(w_in):
    offs = np.cumsum((0,) + IN_SPLITS)
    w16 = w_in.astype(BF16)
    qa, ka, va, za, ba, aa, qb, kcm, vcm, ksl, vsl, kwi, vwi, gnsa, gmix = [
        w16[:, offs[i]:offs[i + 1]] for i in range(len(IN_SPLITS))]
    qb = qb * (NSA_DK ** -0.5)
    grp = lambda y, g: y[:, g * NSA_DK:(g + 1) * NSA_DK]
    kv4 = jnp.concatenate([grp(ksl, 0), grp(vsl, 0), grp(ksl, 1), grp(vsl, 1),
                           grp(kwi, 0), grp(vwi, 0), grp(kwi, 1), grp(vwi, 1)], axis=1)
    used = 2 * GDN_HEADS + NSA_HEADS * 3
    small = jnp.concatenate([ba, aa, gnsa, jnp.zeros((w_in.shape[0], LANES - used), BF16)], axis=1)
    w = jnp.concatenate([qa, ka, va, za, qb, kcm, vcm, kv4, gmix, small], axis=1)
    assert w.shape[1] == sum(PROJ_WIDTHS)
    return w


def _in_proj(x2, norm_w, w_prep, seq):
    t, d = x2.shape
    tm = min(TM_PROJ, seq)
    assert seq % tm == 0 and tm % CMP_STRIDE == 0
    n = w_prep.shape[1]
    per_seq = seq // tm
    ncol = PROJ_WIDTHS[PROJ_CMP] // NSA_DK
    out_specs = [pl.BlockSpec((tm, wd), lambda i: (i, 0)) for wd in PROJ_WIDTHS]
    out_shape = [jax.ShapeDtypeStruct((t, wd), dt) for wd, dt in zip(PROJ_WIDTHS, PROJ_DTYPES)]
    out_specs[PROJ_CMP] = pl.BlockSpec((1, ncol, tm // CMP_STRIDE, CMP_STRIDE * NSA_DK),
                                       lambda i: (i // per_seq, 0, i % per_seq, 0))
    out_shape[PROJ_CMP] = jax.ShapeDtypeStruct((t // seq, ncol, seq // CMP_STRIDE, CMP_STRIDE * NSA_DK),
                                               PROJ_DTYPES[PROJ_CMP])
    return pl.pallas_call(
        _in_proj_kernel,
        grid=(t // tm,),
        in_specs=[pl.BlockSpec((tm, d), lambda i: (i, 0)),
                  _resident((1, d)),
                  _resident((d, n))],
        out_specs=out_specs,
        out_shape=out_shape,
        scratch_shapes=[pltpu.VMEM((PROJ_WIDTHS[PROJ_CMP] // LANES, tm, LANES), F32)],
        compiler_params=pltpu.CompilerParams(dimension_semantics=("arbitrary",), vmem_limit_bytes=VMEM_LIMIT),
        name="in_proj",
    )(x2, norm_w.reshape(1, d), w_prep)


def _gdn_kernel(qkv_ref, z_ref, sm_ref, cw_ref, alog_ref, dtb_ref, nw_ref, o_ref,
                xbuf, qn, kn, vn, gcs, bts, state, *, tc):
    H, DK, C = GDN_HEADS, GDN_DK, GDN_CHUNK

    @pl.when(pl.program_id(1) == 0)
    def _():
        xbuf[0:SUBLANES, :] = jnp.zeros((SUBLANES, xbuf.shape[1]), F32)
        state[...] = jnp.zeros(state.shape, F32)

    xbuf[SUBLANES:SUBLANES + tc, :] = qkv_ref[...]

    for blk in range(3 * H):
        cs = slice(blk * DK, (blk + 1) * DK)
        conv = cw_ref[GDN_CONV - 1:GDN_CONV, cs] * xbuf[SUBLANES:SUBLANES + tc, cs]
        for kk in range(GDN_CONV - 1):
            r0 = SUBLANES - (GDN_CONV - 1) + kk
            conv = conv + cw_ref[kk:kk + 1, cs] * xbuf[r0:r0 + tc, cs]
        act = _silu(conv)
        which, h = divmod(blk, H)
        if which == 0:
            qn[h] = act * lax.rsqrt(jnp.sum(act * act, axis=-1, keepdims=True) + EPS) * (DK ** -0.5)
        elif which == 1:
            kn[h] = act * lax.rsqrt(jnp.sum(act * act, axis=-1, keepdims=True) + EPS)
        else:
            vn[h] = act
    xbuf[0:SUBLANES, :] = xbuf[tc:tc + SUBLANES, :]

    sm = sm_ref[...]
    bts[...] = _sigmoid(sm)
    g_all = -jnp.exp(alog_ref[...]) * _softplus(sm + dtb_ref[...])
    ri = lax.broadcasted_iota(jnp.int32, (tc, tc), 0)
    ci = lax.broadcasted_iota(jnp.int32, (tc, tc), 1)
    same_chunk = (ri >> _log2(C)) == (ci >> _log2(C))
    block_tril = jnp.where(same_chunk & (ri >= ci), 1.0, 0.0).astype(F32)
    gcs[...] = _dot(block_tril, g_all, precision=HIGHEST)

    i64 = lax.broadcasted_iota(jnp.int32, (C, C), 0)
    j64 = lax.broadcasted_iota(jnp.int32, (C, C), 1)
    incl = i64 >= j64
    strict = i64 > j64
    eye = jnp.where(i64 == j64, 1.0, 0.0).astype(F32)
    pair_masks = [((i64 >> (lv + 1)) == (j64 >> (lv + 1))) & ((i64 >> lv) != (j64 >> lv)) for lv in range(_log2(C))]
    nw = nw_ref[...]

    units = [(c, h) for c in range(tc // C) for h in range(H)]
    rows = [slice(c * C, (c + 1) * C) for c in range(tc // C)]
    gc = [gcs[r, :] for r in rows]
    bt = [bts[r, :] for r in rows]
    gc_t = [x.T for x in gc]
    bt_t = [x.T for x in bt]
    gc_col = [gc[c][:, H + h:H + h + 1] for c, h in units]
    g_last = [gc[c][C - 1:C, H + h:H + h + 1] for c, h in units]
    e = [jnp.exp(jnp.where(incl, gc_col[n] - gc_t[c][H + h:H + h + 1, :], 0.0)) for n, (c, h) in enumerate(units)]
    q = [qn[h, rows[c], :] for c, h in units]
    k = [kn[h, rows[c], :] for c, h in units]
    kb = [x.astype(BF16) for x in k]
    kkt = [_dot_nt(x, x) for x in kb]
    qkm = [(_dot_nt(q[n].astype(BF16), kb[n]) * jnp.where(incl, e[n], 0.0)).astype(BF16) for n in range(len(units))]
    low = [bt[c][:, h:h + 1] * kkt[n] * jnp.where(strict, e[n], 0.0) for n, (c, h) in enumerate(units)]
    acc = [-jnp.where(pair_masks[0], x, 0.0) for x in low]
    for pm in pair_masks[1:]:
        off = [jnp.where(pm, x, 0.0) for x in low]
        tb = [off[n] + _dot(acc[n].astype(BF16), off[n].astype(BF16)) for n in range(len(units))]
        acc = [acc[n] - tb[n] - _dot(tb[n].astype(BF16), acc[n].astype(BF16)) for n in range(len(units))]
    t_beta = [((eye + acc[n]) * bt_t[c][h:h + 1, :]).astype(BF16) for n, (c, h) in enumerate(units)]
    gam = [jnp.exp(x) for x in gc_col]
    sol = [_dot(t_beta[n], jnp.concatenate([k[n] * gam[n], vn[h, rows[c], :]], axis=1).astype(BF16))
           for n, (c, h) in enumerate(units)]
    wq = [jnp.concatenate([sol[n][:, :DK], q[n] * gam[n]], axis=0).astype(BF16) for n in range(len(units))]
    kt_t = [(k[n] * jnp.exp(g_last[n] - gc_col[n])).T.astype(BF16) for n in range(len(units))]
    s_decay = [jnp.exp(x) for x in g_last]

    heads = range(H)
    s_cur = [state[h] for h in heads]
    for c in range(tc // C):
        ns = [c * H + h for h in heads]
        ws = [_dot(wq[n], s_cur[h].astype(BF16)) for h, n in zip(heads, ns)]
        ub = [(sol[n][:, DK:] - ws[h][:C]).astype(BF16) for h, n in zip(heads, ns)]
        o = [ws[h][C:] + _dot(qkm[n], ub[h]) for h, n in zip(heads, ns)]
        s_cur = [s_decay[n] * s_cur[h] + _dot(kt_t[n], ub[h]) for h, n in zip(heads, ns)]
        for h in heads:
            y = _rms(o[h], nw) * _silu(z_ref[rows[c], h * GDN_DV:(h + 1) * GDN_DV])
            o_ref[rows[c], h * GDN_DV:(h + 1) * GDN_DV] = y.astype(o_ref.dtype)
    for h in heads:
        state[h] = s_cur[h]


def _gdn(qkv, z, small_a, conv_w, a_log, dt_bias, norm_w, batch, seq):
    t = qkv.shape[0]
    tc = min(TC_GDN, seq)
    nj = seq // tc
    H = GDN_HEADS
    pad = lambda v: jnp.zeros((1, LANES), F32).at[0, H:2 * H].set(v.astype(F32))
    tok = lambda w: pl.BlockSpec((tc, w), lambda b, j: (b * nj + j, 0))
    return pl.pallas_call(
        functools.partial(_gdn_kernel, tc=tc),
        grid=(batch, nj),
        in_specs=[tok(3 * H * GDN_DK), tok(H * GDN_DV), tok(LANES),
                  _resident(conv_w.shape), _resident((1, LANES)), _resident((1, LANES)),
                  _resident((1, GDN_DV))],
        out_specs=tok(H * GDN_DV),
        out_shape=jax.ShapeDtypeStruct((t, H * GDN_DV), BF16),
        scratch_shapes=[pltpu.VMEM((tc + SUBLANES, 3 * H * GDN_DK), F32),
                        pltpu.VMEM((H, tc, GDN_DK), F32),
                        pltpu.VMEM((H, tc, GDN_DK), F32),
                        pltpu.VMEM((H, tc, GDN_DV), F32),
                        pltpu.VMEM((tc, LANES), F32),
                        pltpu.VMEM((tc, LANES), F32),
                        pltpu.VMEM((H, GDN_DK, GDN_DV), F32)],
        compiler_params=pltpu.CompilerParams(dimension_semantics=("arbitrary", "arbitrary"),
                                             vmem_limit_bytes=VMEM_LIMIT),
        name="gdn_mixer",
    )(qkv, z, small_a, conv_w, pad(a_log), pad(dt_bias), norm_w.reshape(1, GDN_DV))


def _compress_kernel(x_ref, pos_ref, w1_ref, w2_ref, o_ref, shift, *, nsub):
    half = CMP_STRIDE * NSA_DK
    x = x_ref[0, 0]
    pos = jnp.broadcast_to(pos_ref[0], (SUBLANES, 2 * half)).astype(BF16)
    first = _dot(x, w1_ref[0, :half, :]) + _dot(pos[:, :half], w1_ref[0, :half, :])[0:1]
    second = _dot(x, w1_ref[0, half:, :]) + _dot(pos[:, half:], w1_ref[0, half:, :])[0:1]
    shift[0:nsub, :] = second
    shift[nsub:nsub + SUBLANES, :] = jnp.zeros((SUBLANES, CMP_HIDDEN), F32)
    hidden = first + shift[1:nsub + 1, :]
    y = _dot(_gelu_tanh(hidden).astype(BF16), w2_ref[0])
    row = lax.broadcasted_iota(jnp.int32, y.shape, 0)
    o_ref[0, 0] = jnp.where(row < nsub - 1, y, 0.0)


def _compress(xs, pos_k, w1_k, w2_k, pos_v, w1_v, w2_v, batch, seq):
    G, d = NSA_GROUPS, NSA_DK
    nsub = seq // CMP_STRIDE
    pos = jnp.stack([pos_k.reshape(1, -1), pos_v.reshape(1, -1)])
    w1 = jnp.stack([w1_k, w1_v]).astype(BF16)
    w2 = jnp.stack([w2_k, w2_v]).astype(BF16)
    out = pl.pallas_call(
        functools.partial(_compress_kernel, nsub=nsub),
        grid=(batch, 2 * G),
        in_specs=[pl.BlockSpec((1, 1, nsub, CMP_STRIDE * d), lambda b, c: (b, c, 0, 0)),
                  pl.BlockSpec((1, 1, CMP_LEN * d), lambda b, c: (c // G, 0, 0)),
                  pl.BlockSpec((1, CMP_LEN * d, CMP_HIDDEN), lambda b, c: (c // G, 0, 0)),
                  pl.BlockSpec((1, CMP_HIDDEN, d), lambda b, c: (c // G, 0, 0))],
        out_specs=pl.BlockSpec((1, 1, nsub, d), lambda b, c: (b, c, 0, 0)),
        out_shape=jax.ShapeDtypeStruct((batch, 2 * G, nsub, d), F32),
        scratch_shapes=[pltpu.VMEM((nsub + SUBLANES, CMP_HIDDEN), F32)],
        compiler_params=pltpu.CompilerParams(dimension_semantics=("arbitrary", "arbitrary"),
                                             vmem_limit_bytes=VMEM_LIMIT),
        name="nsa_compress",
    )(xs, pos, w1, w2)
    return jnp.concatenate([out[:, :G], out[:, G:]], axis=-1).astype(BF16)


NSA_MASK = -1e30
NSA_M0 = -1e29
NSA_VROWS = NSA_DV + 16
SEL, WIN = 0, 1
NSA_ONEHOT_ROW = 80
NSA_BIAS_ROWS = 16


def _nsa_kernel(q_ref, kcv_ref, ksel_ref, kwin_ref, gate_ref, slope_ref, o_ref,
                kc_aug, vc_t, k_all, v_all, q_sc, sbuf, negq, m_all, acc_all, *, tq, seq):
    R, DK, DV = NSA_HPG, NSA_DK, NSA_DV
    nsub = seq // CMP_STRIDE
    nsel = seq // SEL_BLOCK
    topn = min(SEL_TOPN, nsel)
    nql = R * tq
    per_step = tq // SEL_BLOCK
    nstep_sel = nsel // per_step
    pad = WINDOW // tq
    i = pl.program_id(2)
    t0 = i * tq
    d_kq = (lax.broadcasted_iota(jnp.int32, (tq, nql), 0)
            - (lax.broadcasted_iota(jnp.int32, (tq, nql), 1) & (tq - 1)))

    def aug_keys(kv, pos):
        lane = lax.broadcasted_iota(jnp.int32, kv.shape, 1)
        block_in_step = (pos >> _log2(SEL_BLOCK)) & (per_step - 1)
        onehot = jnp.where(lane - NSA_ONEHOT_ROW == block_in_step, 1.0, 0.0)
        aug = jnp.where(lane == DK, (pos >> _log2(LANES)).astype(F32),
                        jnp.where(lane == DK + 1, (pos & (LANES - 1)).astype(F32),
                                  jnp.where(lane == DK + 2, 1.0, onehot)))
        return jnp.where(lane < DK, kv, aug).astype(BF16)

    def values_t(kv):
        t = kv.T
        return jnp.concatenate([t[DK:DK + DV], jnp.ones((NSA_VROWS - DV, kv.shape[0]), F32)], axis=0).astype(BF16)

    @pl.when(i == 0)
    def _():
        lane = lax.broadcasted_iota(jnp.int32, (pad * tq, LANES), 1)
        pad_keys = jnp.where(lane == DK, NSA_MASK, 0.0).astype(BF16)
        for br, src in ((SEL, ksel_ref), (WIN, kwin_ref)):
            k_all[br, 0:pad * tq, :] = pad_keys
            v_all[br, 0:pad] = jnp.zeros((pad, NSA_VROWS, tq), BF16)

        def fill(c, carry):
            r0 = pl.multiple_of(c * tq, tq)
            pos = lax.broadcasted_iota(jnp.int32, (tq, LANES), 0) + r0
            for br, src in ((SEL, ksel_ref), (WIN, kwin_ref)):
                kv = src[pl.ds(r0, tq), :].astype(F32)
                k_all[br, pl.ds(r0 + pad * tq, tq), :] = aug_keys(kv, pos)
                v_all[br, c + pad] = values_t(kv)
            return carry

        lax.fori_loop(0, seq // tq, fill, 0)
        kcv = kcv_ref[0, 0].astype(F32)
        pos_c = lax.broadcasted_iota(jnp.int32, (nsub, LANES), 0) * CMP_STRIDE + (CMP_LEN - 1)
        kc_aug[...] = aug_keys(kcv, pos_c)
        vc_t[...] = kcv.T[DK:DK + DV].astype(BF16)
        negq[nstep_sel] = jnp.zeros((NSA_BIAS_ROWS, nql), BF16)

    row_q = lax.broadcasted_iota(jnp.int32, (LANES - DK, tq), 0)
    t0_f = (jnp.zeros((LANES - DK, tq), jnp.int32) + t0).astype(F32)
    q_all = q_ref[...].astype(F32).T
    qts = []
    for r in range(R):
        sl = slope_ref[0, r:r + 1, 0:1]
        aug = jnp.where(row_q == 0, sl * float(LANES),
                        jnp.where(row_q == 1, sl, jnp.where(row_q == 2, -sl * t0_f, 0.0)))
        qts.append(jnp.concatenate([q_all[r * DK:(r + 1) * DK], aug], axis=0))
    q_t = jnp.concatenate(qts, axis=1).astype(BF16)

    blk_end = lax.broadcasted_iota(jnp.int32, (nsub, nql), 0) * CMP_STRIDE + (CMP_LEN - 1)
    q_pos = (lax.broadcasted_iota(jnp.int32, (nsub, nql), 1) & (tq - 1)) + t0
    s = jnp.where(blk_end <= q_pos, _dot(kc_aug[...], q_t), NSA_MASK)
    m = jnp.maximum(jnp.max(s, axis=0, keepdims=True), NSA_M0)
    e = jnp.exp(s - m)
    den = jnp.sum(e, axis=0, keepdims=True)
    p = e * jnp.where(den > 0.0, 1.0 / den, 0.0)
    o_cmp = _dot(vc_t[...], p.astype(BF16))
    psum = p[:, 0:tq]
    for r in range(1, R):
        psum = psum + p[:, r * tq:(r + 1) * tq]

    nwin = WINDOW // tq + 1
    oh = NSA_ONEHOT_ROW

    def keys_of(br, step):
        return k_all[br, pl.ds(pl.multiple_of(step * tq, tq), tq), :]

    def accumulate(s, br, step):
        m_old = m_all[br]
        m_new = jnp.maximum(m_old, jnp.max(s, axis=0, keepdims=True))
        p = jnp.exp(s - m_new).astype(BF16)
        acc_all[br] = jnp.exp(m_old - m_new) * acc_all[br] + _dot(v_all[br, step], p)
        m_all[br] = m_new

    m_all[...] = jnp.full(m_all.shape, NSA_M0, F32)
    acc_all[...] = jnp.zeros(acc_all.shape, F32)

    s_win = [_dot(keys_of(WIN, i + w), q_t) for w in range(nwin)]
    s_win[0] = jnp.where(d_kq > 0, s_win[0], NSA_MASK)
    s_win[-1] = jnp.where(d_kq <= 0, s_win[-1], NSA_MASK)
    m_win = jnp.max(s_win[0], axis=0, keepdims=True)
    for s_w in s_win[1:]:
        m_win = jnp.maximum(m_win, jnp.max(s_w, axis=0, keepdims=True))
    pv_win = []

    def window_piece(w):
        pv_win.append(_dot(v_all[WIN, i + w], jnp.exp(s_win[w] - m_win).astype(BF16)))

    j_o = lax.broadcasted_iota(jnp.int32, (nsel, nsub), 0) * SEL_BLOCK
    n_o = lax.broadcasted_iota(jnp.int32, (nsel, nsub), 1) * CMP_STRIDE
    ov = jnp.maximum(jnp.minimum(n_o + CMP_LEN, j_o + SEL_BLOCK) - jnp.maximum(n_o, j_o), 0)
    ov = (ov.astype(F32) * (1.0 / CMP_LEN)).astype(BF16)
    p_hi = psum.astype(BF16)
    p_lo = (psum - p_hi.astype(F32)).astype(BF16)
    imp = _dot(ov, p_hi) + _dot(ov, p_lo)
    blk = lax.broadcasted_iota(jnp.int32, (nsel, tq), 0)
    cur = (lax.broadcasted_iota(jnp.int32, (nsel, tq), 1) + t0) >> _log2(SEL_BLOCK)
    valid_s = blk <= cur
    forced = (blk == 0) | (blk == cur) | (blk == cur - 1)
    score = jnp.where(valid_s, jnp.where(forced, FORCED_SCORE, imp), -jnp.inf)
    blk_f = blk.astype(F32)
    npieces = nwin
    done = 0
    for it in range(topn):
        m = jnp.max(score, axis=0, keepdims=True)
        first = jnp.min(jnp.where(score == m, blk_f, float(nsel)), axis=0, keepdims=True)
        score = jnp.where(blk_f == first, -jnp.inf, score)
        while done < ((it + 1) * npieces) // topn:
            window_piece(done)
            done += 1
    acc_all[WIN] = functools.reduce(lambda a, b: a + b, pv_win)
    m_all[WIN] = m_win
    nb = jnp.where(valid_s & (score == -jnp.inf), 0.0, NSA_MASK)
    for n in range(nstep_sel):
        rows = jnp.concatenate([nb[n * per_step:(n + 1) * per_step],
                                jnp.zeros((NSA_BIAS_ROWS - per_step, tq), F32)], axis=0)
        negq[n] = jnp.concatenate([rows] * R, axis=1).astype(BF16)

    q_diag = jnp.concatenate([q_t[:oh], negq[i], q_t[oh + NSA_BIAS_ROWS:]], axis=0)
    accumulate(jnp.where(d_kq <= 0, _dot(keys_of(SEL, i + pad), q_diag), NSA_MASK), SEL, i + pad)

    def schedule(n):
        dummy = n >= i
        return jnp.where(dummy, WIN, SEL), jnp.where(dummy, 0, n + pad), jnp.where(dummy, nstep_sel, n)

    def issue_scores(n, slot):
        br, step, group = schedule(n)
        q_sc[oh:oh + NSA_BIAS_ROWS, :] = negq[group]
        sbuf[slot] = _dot(keys_of(br, step), q_sc[...])

    def consume(n, slot):
        br, step, _ = schedule(n)
        accumulate(sbuf[slot], br, step)

    q_sc[...] = q_t
    issue_scores(0, 0)

    def trip(n, steps):
        for j in range(steps):
            issue_scores(n + j + 1, (j + 1) & 1)
            consume(n + j, j & 1)

    def loop(count, first, steps):
        def body(m, carry):
            trip(first + steps * m, steps)
            return carry
        lax.fori_loop(0, count, body, 0)

    long_trips = i >> 2
    loop(long_trips, 0, 4)
    loop((i - 4 * long_trips + 1) >> 1, 4 * long_trips, 2)
    acc_s = acc_all[SEL]
    acc_w = acc_all[WIN]
    o_sel = acc_s[:DV] / acc_s[DV:DV + 1]
    o_win = acc_w[:DV] / acc_w[DV:DV + 1]

    g_all = _sigmoid(gate_ref[...]).T
    g0 = 2 * GDN_HEADS
    g_t = jnp.where(pl.program_id(1) == 0, g_all[g0:g0 + 3 * R], g_all[g0 + 3 * R:g0 + 6 * R])
    ys = []
    for r in range(R):
        cs = slice(r * tq, (r + 1) * tq)
        ys.append(g_t[3 * r:3 * r + 1] * o_cmp[:, cs] + g_t[3 * r + 1:3 * r + 2] * o_sel[:, cs]
                  + g_t[3 * r + 2:3 * r + 3] * o_win[:, cs])
    for half in range(R // 2):
        pair = jnp.concatenate([ys[2 * half], ys[2 * half + 1]], axis=0)
        o_ref[:, half * LANES:(half + 1) * LANES] = pair.T.astype(o_ref.dtype)


def _nsa_attention(qb, kcv, kv4, small, batch, seq):
    t = qb.shape[0]
    G, R = NSA_GROUPS, NSA_HPG
    tq = TQ_NSA
    assert tq % LANES == 0 and WINDOW % tq == 0 and seq % tq == 0 and LANES == 2 * NSA_DK
    nq = seq // tq
    nsub = seq // CMP_STRIDE
    nsel = seq // SEL_BLOCK
    slopes = 2.0 ** (-8.0 * jnp.arange(1, NSA_HEADS + 1, dtype=F32) / NSA_HEADS)
    slope_rows = jnp.zeros((G, SUBLANES, LANES), F32).at[:, :R, :].set(
        jnp.broadcast_to(slopes.reshape(G, R, 1), (G, R, LANES)))
    return pl.pallas_call(
        functools.partial(_nsa_kernel, tq=tq, seq=seq),
        grid=(batch, G, nq),
        in_specs=[pl.BlockSpec((tq, R * NSA_DK), lambda b, g, i: (b * nq + i, g)),
                  pl.BlockSpec((1, 1, nsub, LANES), lambda b, g, i: (b, g, 0, 0)),
                  pl.BlockSpec((seq, LANES), lambda b, g, i: (b, g)),
                  pl.BlockSpec((seq, LANES), lambda b, g, i: (b, G + g)),
                  pl.BlockSpec((tq, LANES), lambda b, g, i: (b * nq + i, 0)),
                  pl.BlockSpec((1, SUBLANES, LANES), lambda b, g, i: (g, 0, 0))],
        out_specs=pl.BlockSpec((tq, R * NSA_DV), lambda b, g, i: (b * nq + i, g)),
        out_shape=jax.ShapeDtypeStruct((t, NSA_HEADS * NSA_DV), BF16),
        scratch_shapes=[pltpu.VMEM((nsub, LANES), BF16),
                        pltpu.VMEM((NSA_DV, nsub), BF16),
                        pltpu.VMEM((2, seq + WINDOW, LANES), BF16),
                        pltpu.VMEM((2, (seq + WINDOW) // tq, NSA_VROWS, tq), BF16),
                        pltpu.VMEM((LANES, R * tq), BF16),
                        pltpu.VMEM((2, tq, R * tq), F32),
                        pltpu.VMEM((nsel // (tq // SEL_BLOCK) + 1, NSA_BIAS_ROWS, R * tq), BF16),
                        pltpu.VMEM((2, 1, R * tq), F32),
                        pltpu.VMEM((2, NSA_VROWS, R * tq), F32)],
        compiler_params=pltpu.CompilerParams(dimension_semantics=("arbitrary", "arbitrary", "arbitrary"),
                                             vmem_limit_bytes=VMEM_LIMIT),
        name="nsa_attention",
    )(qb, kcv, kv4, kv4, small, slope_rows)


def _tail_kernel(x_ref, ya_ref, yb_ref, gm_ref, p_ref, wa_ref, wb_ref, wo_ref, nmix_ref, nfpre_ref,
                 wup_ref, cw_ref, cb_ref, wdn_ref, nfpost_ref, wple_ref, wpg_ref, nple_ref, o_ref,
                 ext, tail, act, *, tm, seq, d_ff):
    d = x_ref.shape[1]
    halo = FFN_CONV - 1

    @pl.when((pl.program_id(0) * tm) % seq == 0)
    def _():
        tail[...] = jnp.zeros(tail.shape, F32)

    nsub = ext.shape[0]
    ts = tm // nsub
    subs = [slice(s * ts, (s + 1) * ts) for s in range(nsub)]
    h1, u = [], []
    for rows in subs:
        mixed = (gm_ref[rows, :d].astype(F32) * _dot(ya_ref[rows, :], wa_ref[...])
                 + gm_ref[rows, d:].astype(F32) * _dot(yb_ref[rows, :], wb_ref[...]))
        h1.append(x_ref[rows, :] + _rms(_dot(mixed.astype(BF16), wo_ref[...]), nmix_ref[...]))
        u.append(_rms(h1[-1], nfpre_ref[...]).astype(BF16))

    for c in range(d_ff // FC_TAIL):
        fs = [[] for _ in range(nsub)]
        for part in range(2):
            cols = slice(part * d_ff + c * FC_TAIL, part * d_ff + (c + 1) * FC_TAIL)
            for s in range(nsub):
                ext[s, 0:SUBLANES, :] = tail[:, cols] if s == 0 else ext[s - 1, ts:ts + SUBLANES, :]
                ext[s, SUBLANES:SUBLANES + ts, :] = _dot(u[s], wup_ref[:, cols])
                f = cb_ref[:, cols] + cw_ref[halo:halo + 1, cols] * ext[s, SUBLANES:SUBLANES + ts, :]
                for k in range(halo):
                    r0 = SUBLANES - halo + k
                    f = f + cw_ref[k:k + 1, cols] * ext[s, r0:r0 + ts, :]
                fs[s].append(f)
            tail[:, cols] = ext[nsub - 1, ts:ts + SUBLANES, :]
        for s in range(nsub):
            act[s, :, c * FC_TAIL:(c + 1) * FC_TAIL] = (_gelu_tanh(fs[s][0]) * fs[s][1]).astype(BF16)

    for s, rows in enumerate(subs):
        h2 = h1[s] + _rms(_dot(act[s], wdn_ref[...]), nfpost_ref[...])
        e = _dot(p_ref[rows, :].astype(BF16), wple_ref[...]) * _sigmoid(_dot(h2.astype(BF16), wpg_ref[...]))
        o_ref[rows, :] = h2 + _rms(e, nple_ref[...])


def _tail(x2, ya, yb, gmix, p2, w_a2d, w_b2d, w_o, n_mix, n_fpre, w_up, conv_ffn, conv_b, w_down, n_fpost,
          w_ple, w_ple_gate, n_ple, seq):
    t, d = x2.shape
    d_ff = w_down.shape[0]
    tm = min(TM_TAIL, seq)
    tok = lambda w: pl.BlockSpec((tm, w), lambda i: (i, 0))
    row = lambda v: v.reshape(1, -1)
    b16 = lambda w: w.astype(BF16)
    consts = [b16(w_a2d), b16(w_b2d), b16(w_o), row(n_mix), row(n_fpre), b16(w_up), conv_ffn, row(conv_b),
              b16(w_down), row(n_fpost), b16(w_ple), b16(w_ple_gate), row(n_ple)]
    return pl.pallas_call(
        functools.partial(_tail_kernel, tm=tm, seq=seq, d_ff=d_ff),
        grid=(t // tm,),
        in_specs=[tok(d), tok(ya.shape[1]), tok(yb.shape[1]), tok(2 * d), tok(p2.shape[1])]
                 + [_resident(c.shape) for c in consts],
        out_specs=tok(d),
        out_shape=jax.ShapeDtypeStruct((t, d), F32),
        scratch_shapes=[pltpu.VMEM((NSUB_TAIL, tm // NSUB_TAIL + SUBLANES, FC_TAIL), F32),
                        pltpu.VMEM((SUBLANES, 2 * d_ff), F32),
                        pltpu.VMEM((NSUB_TAIL, tm // NSUB_TAIL, d_ff), BF16)],
        compiler_params=pltpu.CompilerParams(dimension_semantics=("arbitrary",), vmem_limit_bytes=VMEM_LIMIT),
        name="tail_ffn",
    )(x2, ya, yb, gmix, p2, *consts)


def _layer(h2, p2, batch, seq, norm_mix_pre, w_in, conv_qkv, a_log, dt_bias, gdn_norm,
           cmp_pos_k, cmp_w1_k, cmp_w2_k, cmp_pos_v, cmp_w1_v, cmp_w2_v,
           w_a2d, w_b2d, w_o, norm_mix_post, norm_ffn_pre, w_up, conv_ffn, conv_ffn_b,
           w_down, norm_ffn_post, w_ple, w_ple_gate, norm_ple_post):
    qkv, z, qb, cmp, kv4, gmix, small = _in_proj(h2, norm_mix_pre, _prep_w_in(w_in), seq)
    ya = _gdn(qkv, z, small, conv_qkv, a_log, dt_bias, gdn_norm, batch, seq)
    kcv = _compress(cmp, cmp_pos_k, cmp_w1_k, cmp_w2_k, cmp_pos_v, cmp_w1_v, cmp_w2_v, batch, seq)
    yb = _nsa_attention(qb, kcv, kv4, small, batch, seq)
    return _tail(h2, ya, yb, gmix, p2, w_a2d, w_b2d, w_o, norm_mix_post, norm_ffn_pre, w_up, conv_ffn,
                 conv_ffn_b, w_down, norm_ffn_post, w_ple, w_ple_gate, norm_ple_post, seq)


def kernel(x, p, norm_mix_pre, w_in, conv_qkv, a_log, dt_bias, gdn_norm, cmp_pos_k, cmp_w1_k, cmp_w2_k, cmp_pos_v, cmp_w1_v, cmp_w2_v, w_a2d, w_b2d, w_o, norm_mix_post, norm_ffn_pre, w_up, conv_ffn, conv_ffn_b, w_down, norm_ffn_post, w_ple, w_ple_gate, norm_ple_post):
    batch, seq, d = x.shape
    params = (norm_mix_pre, w_in, conv_qkv, a_log, dt_bias, gdn_norm, cmp_pos_k, cmp_w1_k, cmp_w2_k,
              cmp_pos_v, cmp_w1_v, cmp_w2_v, w_a2d, w_b2d, w_o, norm_mix_post, norm_ffn_pre, w_up,
              conv_ffn, conv_ffn_b, w_down, norm_ffn_post, w_ple, w_ple_gate, norm_ple_post)
    h = x.reshape(batch * seq, d)
    for i in range(p.shape[0]):
        h = _layer(h, p[i].reshape(batch * seq, -1), batch, seq, *[w[i] for w in params])
    return h.reshape(batch, seq, d)
```
